```python
import jax, jax.numpy as jnp
from jax import lax
import numpy as np

D_MODEL = 1024
BATCH = 4
SEQ = 4096
DEPTH = 1

CHUNK = 64
EPS = 1e-6
D_MIX = D_MODEL
MLSTM_HEADS = 4
MLSTM_HEAD_DIM = (D_MIX // 2) // MLSTM_HEADS
MLSTM_WIDTH = MLSTM_HEADS * MLSTM_HEAD_DIM
CONV_WIDTH = 4
GMLP_GROUPS = 4
GMLP_WIDTH = D_MIX - MLSTM_WIDTH
GMLP_GROUP_DIM = GMLP_WIDTH // GMLP_GROUPS
GMLP_CHUNK = 128
D_IN_PROJ = 4 * MLSTM_WIDTH + 2 * MLSTM_HEADS + 2 * GMLP_WIDTH
PEER_HEADS = 8
PEER_KEYS = 128
PEER_EXPERTS = PEER_KEYS * PEER_KEYS
PEER_QUERY_DIM = 256
PEER_HALF = PEER_QUERY_DIM // 2
PEER_TOPK = 16
PEER_BLOCK = 128

kernel_name = "hymba_mlstm_gmlp_peer_block"


def rmsnorm(x, g):
    xf = x.astype(jnp.float32)
    y = xf * lax.rsqrt(jnp.mean(xf * xf, axis=-1, keepdims=True) + EPS)
    return (y * g.astype(jnp.float32)).astype(x.dtype)


def causal_conv(x, w, b):
    S = x.shape[1]
    xp = jnp.pad(x, ((0, 0), (CONV_WIDTH - 1, 0), (0, 0)))
    out = b
    for j in range(CONV_WIDTH):
        out = out + xp[:, j:j + S] * w[j]
    return out


def mlstm_chunkwise(q, k, v, ig, lf):
    B, H, S, Dh = q.shape
    nc = S // CHUNK

    def to_chunks(a):
        a = a.reshape(B, H, nc, CHUNK, *a.shape[3:])
        return jnp.moveaxis(a, 2, 0)

    causal = jnp.tril(jnp.ones((CHUNK, CHUNK), dtype=bool))

    def step(carry, xs):
        C, n, m = carry
        qc, kc, vc, igc, lfc = xs
        b = jnp.cumsum(lfc, axis=-1)
        D = b[..., :, None] - b[..., None, :] + igc[..., None, :]
        D = jnp.where(causal, D, -jnp.inf)
        inter = b + m[..., None]
        m_t = jnp.maximum(inter, jnp.max(D, axis=-1))
        w = jnp.exp(D - m_t[..., None])
        s = jnp.einsum('bhtd,bhsd->bhts', qc, kc) * w
        sc = jnp.exp(inter - m_t)
        num = jnp.einsum('bhts,bhsd->bhtd', s, vc) + sc[..., None] * jnp.einsum('bhvk,bhtk->bhtv', C, qc)
        den = jnp.sum(s, axis=-1) + sc * jnp.einsum('bhk,bhtk->bht', n, qc)
        h = num / jnp.maximum(jnp.abs(den), jnp.exp(-m_t))[..., None]
        bL = b[..., -1]
        g = bL[..., None] - b + igc
        m_new = jnp.maximum(bL + m, jnp.max(g, axis=-1))
        decay = jnp.exp(bL + m - m_new)
        wg = jnp.exp(g - m_new[..., None])
        C = decay[..., None, None] * C + jnp.einsum('bhs,bhsv,bhsk->bhvk', wg, vc, kc)
        n = decay[..., None] * n + jnp.einsum('bhs,bhsk->bhk', wg, kc)
        return (C, n, m_new), h

    init = (jnp.zeros((B, H, Dh, Dh), jnp.float32),
            jnp.zeros((B, H, Dh), jnp.float32),
            jnp.zeros((B, H), jnp.float32))
    _, hs = lax.scan(step, init, (to_chunks(q), to_chunks(k), to_chunks(v), to_chunks(ig), to_chunks(lf)))
    return jnp.moveaxis(hs, 0, 2).reshape(B, H, S, Dh)


def mlstm_group(q_pre, k_pre, v, o_pre, i_pre, f_pre, conv_w, conv_b, b_igate, b_fgate, norm_g):
    B, S, _ = q_pre.shape
    qk = jax.nn.silu(causal_conv(jnp.concatenate([q_pre, k_pre], axis=-1), conv_w, conv_b))
    q, k = qk[..., :MLSTM_WIDTH], qk[..., MLSTM_WIDTH:]

    def heads(a):
        return a.reshape(B, S, MLSTM_HEADS, MLSTM_HEAD_DIM).transpose(0, 2, 1, 3).astype(jnp.float32)

    ig = (i_pre + b_igate).astype(jnp.float32).transpose(0, 2, 1)
    lf = jax.nn.log_sigmoid((f_pre + b_fgate).astype(jnp.float32)).transpose(0, 2, 1)
    h = mlstm_chunkwise(heads(q), heads(k) * (MLSTM_HEAD_DIM ** -0.5), heads(v), ig, lf)
    h = h * lax.rsqrt(jnp.mean(h * h, axis=-1, keepdims=True) + EPS)
    h = h.transpose(0, 2, 1, 3).reshape(B, S, MLSTM_WIDTH) * norm_g.astype(jnp.float32)
    return (h * jax.nn.sigmoid(o_pre.astype(jnp.float32))).astype(q_pre.dtype)


def gmlp_group(u_pre, v_pre, ln_g, ln_b, w_s, b_s):
    B, S, _ = u_pre.shape
    u = jax.nn.gelu(u_pre)
    vv = jax.nn.gelu(v_pre).reshape(B, S, GMLP_GROUPS, GMLP_GROUP_DIM).astype(jnp.float32)
    mu = jnp.mean(vv, axis=-1, keepdims=True)
    var = jnp.mean(jnp.square(vv - mu), axis=-1, keepdims=True)
    vv = (vv - mu) * lax.rsqrt(var + EPS)
    vv = vv * ln_g.reshape(GMLP_GROUPS, GMLP_GROUP_DIM) + ln_b.reshape(GMLP_GROUPS, GMLP_GROUP_DIM)
    vv = vv.astype(u_pre.dtype).reshape(B, S // GMLP_CHUNK, GMLP_CHUNK, GMLP_GROUPS, GMLP_GROUP_DIM)
    pos = jnp.arange(GMLP_CHUNK) // CHUNK
    mask = (pos[:, None] >= pos[None, :]).astype(w_s.dtype)
    mix = jnp.einsum('gts,bcsgd->bctgd', w_s * mask, vv) + b_s.T[:, :, None]
    return u * mix.reshape(B, S, GMLP_WIDTH)


def peer(x, w_query, sub_keys, expert_u, expert_v):
    B, S, D = x.shape
    T = B * S
    xt = x.reshape(T, D)
    q = (xt @ w_query).reshape(T, PEER_HEADS, 2, PEER_HALF)
    s = jnp.einsum('thpd,hpkd->thpk', q, sub_keys).astype(jnp.float32)
    s1, i1 = lax.top_k(s[:, :, 0], PEER_TOPK)
    s2, i2 = lax.top_k(s[:, :, 1], PEER_TOPK)
    cand = (s1[..., :, None] + s2[..., None, :]).reshape(T, PEER_HEADS, PEER_TOPK * PEER_TOPK)
    cand_idx = (i1[..., :, None] * PEER_KEYS + i2[..., None, :]).reshape(T, PEER_HEADS, PEER_TOPK * PEER_TOPK)
    top_s, pos = lax.top_k(cand, PEER_TOPK)
    idx = jnp.take_along_axis(cand_idx, pos, axis=-1)
    gate = jax.nn.softmax(top_s, axis=-1).astype(x.dtype)
    nb = T // PEER_BLOCK

    def block(args):
        xb, ib, gb = args
        ub = jnp.take(expert_u, ib, axis=0)
        vb = jnp.take(expert_v, ib, axis=0)
        a = jax.nn.gelu(jnp.einsum('td,thkd->thk', xb, ub))
        return jnp.einsum('thk,thkd->td', a * gb, vb)

    out = lax.map(block, (xt.reshape(nb, PEER_BLOCK, D),
                          idx.reshape(nb, PEER_BLOCK, PEER_HEADS, PEER_TOPK),
                          gate.reshape(nb, PEER_BLOCK, PEER_HEADS, PEER_TOPK)))
    return out.reshape(B, S, D)


def setup_inputs(seed: int = 0) -> dict:
    key = jax.random.key(seed)
    ks = jax.random.split(key, 20)
    f32 = jnp.float32
    L = DEPTH
    nrm = lambda k, shape, scale: jax.random.normal(k, shape, f32) * scale
    return {
        "x": nrm(ks[0], (BATCH, SEQ, D_MODEL), 1.0),
        "norm1_g": 1.0 + nrm(ks[1], (L, D_MODEL), 0.02),
        "w_in": nrm(ks[2], (L, D_MODEL, D_IN_PROJ), D_MODEL ** -0.5),
        "conv_w": nrm(ks[3], (L, CONV_WIDTH, 2 * MLSTM_WIDTH), CONV_WIDTH ** -0.5),
        "conv_b": nrm(ks[4], (L, 2 * MLSTM_WIDTH), 0.02),
        "b_igate": nrm(ks[5], (L, MLSTM_HEADS), 0.1),
        "b_fgate": jnp.broadcast_to(jnp.linspace(3.0, 6.0, MLSTM_HEADS, dtype=f32), (L, MLSTM_HEADS)) + nrm(ks[6], (L, MLSTM_HEADS), 0.1),
        "mlstm_norm_g": 1.0 + nrm(ks[7], (L, MLSTM_WIDTH), 0.02),
        "gmlp_ln_g": 1.0 + nrm(ks[8], (L, GMLP_WIDTH), 0.02),
        "gmlp_ln_b": nrm(ks[9], (L, GMLP_WIDTH), 0.02),
        "gmlp_w_s": nrm(ks[10], (L, GMLP_GROUPS, GMLP_CHUNK, GMLP_CHUNK), GMLP_CHUNK ** -0.5),
        "gmlp_b_s": 1.0 + nrm(ks[11], (L, GMLP_GROUPS, GMLP_CHUNK), 0.02),
        "w_out": nrm(ks[12], (L, D_MIX, D_MODEL), D_MIX ** -0.5),
        "norm2_g": 1.0 + nrm(ks[13], (L, D_MODEL), 0.02),
        "peer_w_query": nrm(ks[14], (L, D_MODEL, PEER_HEADS * PEER_QUERY_DIM), D_MODEL ** -0.5),
        "peer_sub_keys": nrm(ks[15], (L, PEER_HEADS, 2, PEER_KEYS, PEER_HALF), PEER_HALF ** -0.5),
        "peer_u": nrm(ks[16], (L, PEER_EXPERTS, D_MODEL), D_MODEL ** -0.5),
        "peer_v": nrm(ks[17], (L, PEER_EXPERTS, D_MODEL), PEER_HEADS ** -0.5),
        "final_g": 1.0 + nrm(ks[18], (D_MODEL,), 0.02),
    }


def reference(x, norm1_g, w_in, conv_w, conv_b, b_igate, b_fgate, mlstm_norm_g,
              gmlp_ln_g, gmlp_ln_b, gmlp_w_s, gmlp_b_s, w_out, norm2_g,
              peer_w_query, peer_sub_keys, peer_u, peer_v, final_g):
    W = MLSTM_WIDTH
    for l in range(DEPTH):
        h = rmsnorm(x, norm1_g[l])
        z = h @ w_in[l]
        q_pre = z[..., 0:W]
        k_pre = z[..., W:2 * W]
        v = z[..., 2 * W:3 * W]
        o_pre = z[..., 3 * W:4 * W]
        g0 = 4 * W
        i_pre = z[..., g0:g0 + MLSTM_HEADS]
        f_pre = z[..., g0 + MLSTM_HEADS:g0 + 2 * MLSTM_HEADS]
        u0 = g0 + 2 * MLSTM_HEADS
        u_pre = z[..., u0:u0 + GMLP_WIDTH]
        vg_pre = z[..., u0 + GMLP_WIDTH:u0 + 2 * GMLP_WIDTH]
        y_a = mlstm_group(q_pre, k_pre, v, o_pre, i_pre, f_pre, conv_w[l], conv_b[l],
                          b_igate[l], b_fgate[l], mlstm_norm_g[l])
        y_b = gmlp_group(u_pre, vg_pre, gmlp_ln_g[l], gmlp_ln_b[l], gmlp_w_s[l], gmlp_b_s[l])
        x = x + jnp.concatenate([y_a, y_b], axis=-1) @ w_out[l]
        x = x + peer(rmsnorm(x, norm2_g[l]), peer_w_query[l], peer_sub_keys[l], peer_u[l], peer_v[l])
    return rmsnorm(x, final_g)
```

```python
import functools

import jax
import jax.numpy as jnp
from jax import lax
from jax.experimental import pallas as pl
from jax.experimental.pallas import tpu as pltpu

F32 = jnp.float32
BF16 = jnp.bfloat16
EPS = 1e-6

CHUNK = 64
HEADS = 4
HEAD_DIM = 128
MLSTM_W = HEADS * HEAD_DIM
CONV_K = 4
GROUPS = 4
GROUP_DIM = 128
GMLP_W = GROUPS * GROUP_DIM
GMLP_CHUNK = 128
PEER_HEADS = 8
PEER_KEYS = 128
PEER_HALF = 128
TOPK = 16
NO_RANK = 127.0

VMEM_LIMIT = 56 * 1024 * 1024


def _params(sem):
    return pltpu.CompilerParams(dimension_semantics=sem, vmem_limit_bytes=VMEM_LIMIT)


def _rms(x, g):
    return x * lax.rsqrt(jnp.mean(x * x, axis=-1, keepdims=True) + EPS) * g


def _inproj_kernel(x_ref, g_ref, w_ref, z_ref, xn_ref):
    @pl.when(pl.program_id(1) == 0)
    def _():
        xn_ref[...] = _rms(x_ref[...], g_ref[...]).astype(BF16)

    z_ref[...] = jnp.dot(xn_ref[...], w_ref[...], preferred_element_type=F32)


def _inproj(x, g, w, tm, tn):
    T, D = x.shape
    N = w.shape[1]
    return pl.pallas_call(
        _inproj_kernel,
        grid=(T // tm, N // tn),
        in_specs=[pl.BlockSpec((tm, D), lambda i, j: (i, 0)),
                  pl.BlockSpec((1, D), lambda i, j: (0, 0)),
                  pl.BlockSpec((D, tn), lambda i, j: (0, j))],
        out_specs=pl.BlockSpec((tm, tn), lambda i, j: (i, j)),
        out_shape=jax.ShapeDtypeStruct((T, N), F32),
        scratch_shapes=[pltpu.VMEM((tm, D), BF16)],
        compiler_params=_params(("parallel", "arbitrary")),
        name="inproj",
    )(x, g, w)


def _gate_kernel(i_ref, f_ref, bi_ref, bf_ref,
                 cola_ref, sc_ref, emt_ref, wg_ref, r_ref, decay_ref, *, nchunks):
    ig = i_ref[...] + bi_ref[...]
    lf = jax.nn.log_sigmoid(f_ref[...] + bf_ref[...])
    L, NC = ig.shape
    row = lax.broadcasted_iota(jnp.int32, (L, NC), 0)

    b = lf
    sh = 1
    while sh < L:
        b = b + jnp.where(row >= sh, pltpu.roll(b, sh, axis=0), 0.0)
        sh *= 2
    r = ig - b
    cm = r
    sh = 1
    while sh < L:
        cm = jnp.maximum(cm, jnp.where(row >= sh, pltpu.roll(cm, sh, axis=0), -jnp.inf))
        sh *= 2

    bl = jnp.broadcast_to(b[L - 1:L, :], (8, NC))
    gmax = bl + jnp.broadcast_to(cm[L - 1:L, :], (8, NC))
    lane = lax.broadcasted_iota(jnp.int32, (8, NC), 1)
    first = (lane % nchunks) == 0

    def step(_, m_new):
        m_prev = jnp.where(first, 0.0, pltpu.roll(m_new, 1, axis=1))
        return jnp.maximum(bl + m_prev, gmax)

    m_new = lax.fori_loop(0, nchunks, step, gmax)
    m_prev = jnp.where(first, 0.0, pltpu.roll(m_new, 1, axis=1))

    inter = b + m_prev[0:1, :]
    m_t = jnp.maximum(inter, b + cm)
    cola_ref[...] = b - m_t
    sc_ref[...] = jnp.exp(inter - m_t)
    emt_ref[...] = jnp.exp(-m_t)
    wg_ref[...] = jnp.exp(bl[0:1, :] + r - m_new[0:1, :])
    r_ref[...] = r
    decay_ref[...] = jnp.exp(bl + m_prev - m_new)


def _gates(icol, fcol, bi_row, bf_row, nchunks):
    L, NC = icol.shape
    full = pl.BlockSpec((L, NC), lambda: (0, 0))
    row = pl.BlockSpec((1, NC), lambda: (0, 0))
    row8 = pl.BlockSpec((8, NC), lambda: (0, 0))
    return pl.pallas_call(
        functools.partial(_gate_kernel, nchunks=nchunks),
        in_specs=[full, full, row, row],
        out_specs=[full] * 5 + [row8],
        out_shape=[jax.ShapeDtypeStruct((L, NC), F32)] * 5 + [jax.ShapeDtypeStruct((8, NC), F32)],
        name="mlstm_gates",
    )(icol, fcol, bi_row, bf_row)


def _mlstm_kernel(q_ref, k_ref, v_ref, o_ref, cs_ref, rs_ref, cw_ref, cb_ref, ng_ref, y_ref,
                  caug_ref, hq_ref, hk_ref, *, nck):
    L, H, Dh, W = CHUNK, HEADS, HEAD_DIM, MLSTM_W
    sblk = nck * L

    @pl.when(pl.program_id(1) == 0)
    def _():
        caug_ref[...] = jnp.zeros_like(caug_ref)
        hq_ref[...] = jnp.zeros_like(hq_ref)
        hk_ref[...] = jnp.zeros_like(hk_ref)

    causal = (lax.broadcasted_iota(jnp.int32, (L, L), 0) >= lax.broadcasted_iota(jnp.int32, (L, L), 1))
    lane0 = lax.broadcasted_iota(jnp.int32, (L, Dh), 1) == 0

    def conv_silu(x_ref, hist_ref, c, cw, cb):
        xc = x_ref[pl.ds(pl.multiple_of(c * L, L), L), :]
        start = pl.multiple_of(jnp.maximum(c * L - 8, 0), 8)
        prev = jnp.where(c == 0, hist_ref[...], x_ref[pl.ds(start, 8), :])
        ext = jnp.concatenate([prev, xc], axis=0)
        out = cb
        for j in range(CONV_K):
            sh = CONV_K - 1 - j
            xs = ext if sh == 0 else pltpu.roll(ext, sh, axis=0)
            out = out + xs[8:, :] * cw[j:j + 1, :]
        return out * jax.nn.sigmoid(out)

    def body(c, carry):
        rows = pl.ds(pl.multiple_of(c * L, L), L)
        q = conv_silu(q_ref, hq_ref, c, cw_ref[:, :W], cb_ref[:, :W])
        k = conv_silu(k_ref, hk_ref, c, cw_ref[:, W:], cb_ref[:, W:]) * (Dh ** -0.5)
        v = v_ref[rows, :]
        kt = k.T
        cs = cs_ref[c]
        rs = rs_ref[c]
        outs = []
        for h in range(H):
            sl = slice(h * Dh, (h + 1) * Dh)
            qh = q[:, sl].astype(BF16)
            kh = k[:, sl].astype(BF16)
            kth = kt[sl, :].astype(BF16)
            vh = v[:, sl]
            cola = cs[:, h:h + 1]
            sc = cs[:, H + h:H + h + 1]
            emt = cs[:, 2 * H + h:2 * H + h + 1]
            wg = cs[:, 3 * H + h:3 * H + h + 1]
            rowr = rs[h:h + 1, :L]
            decay = rs[H + h:H + h + 1, :]
            sqk = lax.dot_general(qh, kh, (((1,), (1,)), ((), ())), preferred_element_type=F32)
            w = jnp.exp(jnp.where(causal, cola + rowr, -jnp.inf))
            smat = sqk * w
            den_i = jnp.sum(smat, axis=-1, keepdims=True)
            caug = caug_ref[h]
            qc = jnp.dot(qh, caug.astype(BF16), preferred_element_type=F32)
            num = jnp.dot(smat.astype(BF16), vh.astype(BF16), preferred_element_type=F32) + sc * qc[:, :Dh]
            den = den_i + sc * qc[:, Dh:Dh + 1]
            hh = num / jnp.maximum(jnp.abs(den), emt)
            outs.append(hh * lax.rsqrt(jnp.mean(hh * hh, axis=-1, keepdims=True) + EPS))
            vaug = jnp.concatenate([wg * vh, jnp.where(lane0, wg, 0.0)], axis=1).astype(BF16)
            dc = jnp.dot(kth, vaug, preferred_element_type=F32)
            caug_ref[h] = decay * caug + dc
        hcat = jnp.concatenate(outs, axis=1)
        y = hcat * ng_ref[...] * jax.nn.sigmoid(o_ref[rows, :])
        y_ref[rows, :] = y.astype(y_ref.dtype)
        return carry

    lax.fori_loop(0, nck, body, 0)
    hq_ref[...] = q_ref[pl.ds(sblk - 8, 8), :]
    hk_ref[...] = k_ref[pl.ds(sblk - 8, 8), :]


def _mlstm(z, cs, rs, conv_w, conv_b, norm_g, B, S, nck):
    T = z.shape[0]
    L, H, Dh, W = CHUNK, HEADS, HEAD_DIM, MLSTM_W
    sblk = nck * L
    nsb = S // sblk

    def zspec(col):
        return pl.BlockSpec((sblk, W), lambda b, s, col=col: (b * nsb + s, col))

    return pl.pallas_call(
        functools.partial(_mlstm_kernel, nck=nck),
        grid=(B, nsb),
        in_specs=[zspec(0), zspec(1), zspec(2), zspec(3),
                  pl.BlockSpec((None, nck, L, 4 * H), lambda b, s: (b, s, 0, 0)),
                  pl.BlockSpec((None, nck, 2 * H, 2 * Dh), lambda b, s: (b, s, 0, 0)),
                  pl.BlockSpec((CONV_K, 2 * W), lambda b, s: (0, 0)),
                  pl.BlockSpec((1, 2 * W), lambda b, s: (0, 0)),
                  pl.BlockSpec((1, W), lambda b, s: (0, 0))],
        out_specs=pl.BlockSpec((sblk, W), lambda b, s: (b * nsb + s, 0)),
        out_shape=jax.ShapeDtypeStruct((T, W), BF16),
        scratch_shapes=[pltpu.VMEM((H, Dh, 2 * Dh), F32),
                        pltpu.VMEM((8, W), F32),
                        pltpu.VMEM((8, W), F32)],
        compiler_params=_params(("parallel", "arbitrary")),
        name="mlstm",
    )(z, z, z, z, cs, rs, conv_w, conv_b, norm_g)


def _gmlp_kernel(u_ref, v_ref, lg_ref, lb_ref, ws_ref, bs_ref, y_ref, *, nchunk):
    C, G, Dg = GMLP_CHUNK, GROUPS, GROUP_DIM
    pos_t = lax.broadcasted_iota(jnp.int32, (C, C), 0) // CHUNK
    pos_s = lax.broadcasted_iota(jnp.int32, (C, C), 1) // CHUNK
    mask = (pos_t >= pos_s).astype(F32)
    for g in range(G):
        wm = (ws_ref[g] * mask).astype(BF16)
        cols = slice(g * Dg, (g + 1) * Dg)
        for c in range(nchunk):
            rows = slice(c * C, (c + 1) * C)
            vv = jax.nn.gelu(v_ref[rows, cols])
            mu = jnp.mean(vv, axis=-1, keepdims=True)
            var = jnp.mean(jnp.square(vv - mu), axis=-1, keepdims=True)
            vv = (vv - mu) * lax.rsqrt(var + EPS)
            vv = vv * lg_ref[:, cols] + lb_ref[:, cols]
            mix = jnp.dot(wm, vv.astype(BF16), preferred_element_type=F32) + bs_ref[g]
            y_ref[rows, cols] = (jax.nn.gelu(u_ref[rows, cols]) * mix).astype(y_ref.dtype)


def _gmlp(z, ln_g, ln_b, w_s, bs_b, tg):
    T = z.shape[0]
    W, C, G = GMLP_W, GMLP_CHUNK, GROUPS
    return pl.pallas_call(
        functools.partial(_gmlp_kernel, nchunk=tg // C),
        grid=(T // tg,),
        in_specs=[pl.BlockSpec((tg, W), lambda i: (i, 4)),
                  pl.BlockSpec((tg, W), lambda i: (i, 5)),
                  pl.BlockSpec((1, W), lambda i: (0, 0)),
                  pl.BlockSpec((1, W), lambda i: (0, 0)),
                  pl.BlockSpec((G, C, C), lambda i: (0, 0, 0)),
                  pl.BlockSpec((G, C, GROUP_DIM), lambda i: (0, 0, 0))],
        out_specs=pl.BlockSpec((tg, W), lambda i: (i, 0)),
        out_shape=jax.ShapeDtypeStruct((T, W), BF16),
        compiler_params=_params(("parallel",)),
        name="gmlp",
    )(z, z, ln_g, ln_b, w_s, bs_b)


def _outproj_kernel(x_ref, ya_ref, yb_ref, wa_ref, wb_ref, g_ref, x1_ref, xt_ref):
    x1 = (x_ref[...]
          + jnp.dot(ya_ref[...], wa_ref[...], preferred_element_type=F32)
          + jnp.dot(yb_ref[...], wb_ref[...], preferred_element_type=F32))
    x1_ref[...] = x1
    xt_ref[...] = _rms(x1, g_ref[...]).T.astype(BF16)


def _outproj(x, ya, yb, wa, wb, g, tm):
    T, D = x.shape
    Wh = ya.shape[1]
    return pl.pallas_call(
        _outproj_kernel,
        grid=(T // tm,),
        in_specs=[pl.BlockSpec((tm, D), lambda i: (i, 0)),
                  pl.BlockSpec((tm, Wh), lambda i: (i, 0)),
                  pl.BlockSpec((tm, Wh), lambda i: (i, 0)),
                  pl.BlockSpec((Wh, D), lambda i: (0, 0)),
                  pl.BlockSpec((Wh, D), lambda i: (0, 0)),
                  pl.BlockSpec((1, D), lambda i: (0, 0))],
        out_specs=[pl.BlockSpec((tm, D), lambda i: (i, 0)),
                   pl.BlockSpec((D, tm), lambda i: (0, i))],
        out_shape=[jax.ShapeDtypeStruct((T, D), F32), jax.ShapeDtypeStruct((D, T), BF16)],
        compiler_params=_params(("parallel",)),
        name="outproj",
    )(x, ya, yb, wa, wb, g)


def _topk_kernel(xt_ref, wq_ref, keys_ref, r2_ref, e2_ref, n1_ref, c1_ref,
                 qt_ref, s_ref, a_ref, b_ref, cnt_ref, iz_ref, *, tm):
    NH, K = PEER_HEADS, PEER_KEYS
    qt_ref[...] = jnp.dot(wq_ref[...], xt_ref[...], preferred_element_type=F32).astype(BF16)
    for hp in range(2 * NH):
        s_ref[hp] = jnp.dot(keys_ref[hp], qt_ref[hp * PEER_HALF:(hp + 1) * PEER_HALF, :],
                            preferred_element_type=F32)

    pairs = [(i, j) for i in range(TOPK) for j in range(TOPK) if (i + 1) * (j + 1) <= TOPK]

    def group(gi, carry):
        lanes = pl.ds(pl.multiple_of(gi * 128, 128), 128)
        for h in range(NH):
            for p in range(2):
                cur = s_ref[2 * h + p, :, lanes]
                rank = jnp.full(cur.shape, NO_RANK, F32)
                for j in range(TOPK):
                    m = jnp.max(cur, axis=0, keepdims=True)
                    eq = cur == m
                    if p == 0:
                        a_ref[j, h:h + 1, :] = m
                    else:
                        b_ref[j, h:h + 1, :] = m
                        rank = jnp.where(eq, float(j), rank)
                    cur = jnp.where(eq, -jnp.inf, cur)
                if p == 1:
                    r2_ref[h, :, lanes] = rank
        a = [a_ref[i] for i in range(TOPK)]
        b = [b_ref[j] for j in range(TOPK)]
        cand = [a[i] + b[j] for (i, j) in pairs]
        tau = jnp.full(cand[0].shape, -jnp.inf, F32)
        for cp in cand:
            n_ge = jnp.zeros(cp.shape, F32)
            for cq in cand:
                n_ge = n_ge + jnp.where(cq >= cp, 1.0, 0.0)
            tau = jnp.maximum(tau, jnp.where(n_ge >= float(TOPK), cp, -jnp.inf))
        zsum = jnp.zeros(tau.shape, F32)
        cnt = [jnp.zeros(tau.shape, F32) for _ in range(TOPK)]
        for (i, j), cp in zip(pairs, cand):
            sel = cp >= tau
            cnt[i] = cnt[i] + jnp.where(sel, 1.0, 0.0)
            zsum = zsum + jnp.where(sel, jnp.exp(cp - cand[0]), 0.0)
        for i in range(TOPK):
            cnt_ref[i] = cnt[i]
        iz_ref[...] = 1.0 / zsum
        for h in range(NH):
            s1 = s_ref[2 * h, :, lanes]
            n1 = jnp.zeros(s1.shape, F32)
            for i in range(TOPK):
                n1 = n1 + jnp.where(s1 == a_ref[i, h:h + 1, :], cnt_ref[i, h:h + 1, :], 0.0)
            n1_ref[h, :, lanes] = n1
            c1_ref[h, :, lanes] = jnp.exp(s1 - a_ref[0, h:h + 1, :]) * iz_ref[h:h + 1, :]
            s2 = s_ref[2 * h + 1, :, lanes]
            e2_ref[h, :, lanes] = jnp.exp(s2 - b_ref[0, h:h + 1, :])
        return carry

    lax.fori_loop(0, tm // 128, group, 0)


def _topk(xt, wq_t, keys, tm):
    D, T = xt.shape
    NH, K = PEER_HEADS, PEER_KEYS
    stat = pl.BlockSpec((NH, K, tm), lambda i: (0, 0, i))
    return pl.pallas_call(
        functools.partial(_topk_kernel, tm=tm),
        grid=(T // tm,),
        in_specs=[pl.BlockSpec((D, tm), lambda i: (0, i)),
                  pl.BlockSpec(wq_t.shape, lambda i: (0, 0)),
                  pl.BlockSpec(keys.shape, lambda i: (0, 0, 0))],
        out_specs=[stat] * 4,
        out_shape=[jax.ShapeDtypeStruct((NH, K, T), F32)] * 4,
        scratch_shapes=[pltpu.VMEM((wq_t.shape[0], tm), BF16),
                        pltpu.VMEM((2 * NH, K, tm), F32),
                        pltpu.VMEM((TOPK, NH, 128), F32),
                        pltpu.VMEM((TOPK, NH, 128), F32),
                        pltpu.VMEM((TOPK, NH, 128), F32),
                        pltpu.VMEM((NH, 128), F32)],
        compiler_params=_params(("parallel",)),
        name="peer_topk",
    )(xt, wq_t, keys)


def _peer_kernel(xt_ref, u_ref, vt_ref, r2_ref, e2_ref, n1_ref, c1_ref, x1_ref, fg_ref, o_ref,
                 acc_ref, pt_ref, *, te, tm):
    NH, K = PEER_HEADS, PEER_KEYS
    e = pl.program_id(1)

    @pl.when(e == 0)
    def _():
        acc_ref[...] = jnp.zeros_like(acc_ref)

    at = jnp.dot(u_ref[...], xt_ref[...], preferred_element_type=F32)
    n_i1 = te // K
    i1_rows = pl.ds(pl.multiple_of(e * n_i1, n_i1), n_i1)
    for lg in range(tm // 128):
        lanes = slice(lg * 128, (lg + 1) * 128)
        n1 = [n1_ref[h, i1_rows, lanes] for h in range(NH)]
        c1 = [c1_ref[h, i1_rows, lanes] for h in range(NH)]
        for kk in range(n_i1):
            gate = jnp.zeros((K, 128), F32)
            for h in range(NH):
                sel = r2_ref[h, :, lanes] < n1[h][kk:kk + 1, :]
                gate = gate + jnp.where(sel, e2_ref[h, :, lanes], 0.0) * c1[h][kk:kk + 1, :]
            act = jax.nn.gelu(at[kk * K:(kk + 1) * K, lanes])
            pt_ref[kk * K:(kk + 1) * K, lanes] = (act * gate).astype(BF16)
    acc_ref[...] += jnp.dot(vt_ref[...], pt_ref[...], preferred_element_type=F32)

    @pl.when(e == pl.num_programs(1) - 1)
    def _():
        o_ref[...] = _rms(x1_ref[...] + acc_ref[...].T, fg_ref[...])


def _peer(xt, u, vt, r2, e2, n1, c1, x1, fg, tm, te):
    D, T = xt.shape
    E = u.shape[0]
    NH, K = PEER_HEADS, PEER_KEYS
    assert te == 8 * K, "a step must cover exactly one sublane tile of first-half keys"
    stat = pl.BlockSpec((NH, K, tm), lambda i, e: (0, 0, i))
    return pl.pallas_call(
        functools.partial(_peer_kernel, te=te, tm=tm),
        grid=(T // tm, E // te),
        in_specs=[pl.BlockSpec((D, tm), lambda i, e: (0, i)),
                  pl.BlockSpec((te, D), lambda i, e: (e, 0)),
                  pl.BlockSpec((D, te), lambda i, e: (0, e)),
                  stat, stat, stat, stat,
                  pl.BlockSpec((tm, D), lambda i, e: (i, 0)),
                  pl.BlockSpec((1, D), lambda i, e: (0, 0))],
        out_specs=pl.BlockSpec((tm, D), lambda i, e: (i, 0)),
        out_shape=jax.ShapeDtypeStruct((T, D), F32),
        scratch_shapes=[pltpu.VMEM((D, tm), F32), pltpu.VMEM((te, tm), BF16)],
        compiler_params=_params(("parallel", "arbitrary")),
        name="peer_dense",
    )(xt, u, vt, r2, e2, n1, c1, x1, fg)


def _tiles(T, S):
    def pick(n, pref):
        t = min(pref, n)
        while n % t:
            t //= 2
        return t
    return dict(tm_in=pick(T, 512), tn_in=640, nck=pick(S // CHUNK, 16), tg=pick(T, 512),
                tm_out=pick(T, 512), tm_topk=pick(T, 512), tm_peer=pick(T, 512), te=8 * PEER_KEYS)


def _layer(x2d, B, S, norm1_g, w_in, conv_w, conv_b, b_igate, b_fgate, mlstm_norm_g,
           gmlp_ln_g, gmlp_ln_b, gmlp_w_s, gmlp_b_s, w_out, norm2_g,
           peer_w_query, peer_sub_keys, peer_u, peer_v, out_g):
    T, D = x2d.shape
    H, L, W = HEADS, CHUNK, MLSTM_W
    nc = S // L
    t = _tiles(T, S)

    g0 = 4 * W
    u0 = g0 + 2 * H
    w_r = jnp.concatenate([w_in[:, :g0], w_in[:, u0:], w_in[:, g0:u0],
                           jnp.zeros((D, 128 - 2 * H), w_in.dtype)], axis=1).astype(BF16)
    z = _inproj(x2d, norm1_g[None, :], w_r, t["tm_in"], t["tn_in"])

    gcols = z[:, 6 * W:6 * W + 2 * H].reshape(B, nc, L, 2, H).transpose(3, 2, 0, 4, 1).reshape(2, L, B * H * nc)
    bi_row = jnp.broadcast_to(b_igate[None, :, None], (B, H, nc)).reshape(1, -1)
    bf_row = jnp.broadcast_to(b_fgate[None, :, None], (B, H, nc)).reshape(1, -1)
    cola, sc, emt, wg, r, decay = _gates(gcols[0], gcols[1], bi_row, bf_row, nc)
    cs = jnp.stack([cola, sc, emt, wg]).reshape(4, L, B, H, nc).transpose(2, 4, 1, 0, 3).reshape(B, nc, L, 4 * H)
    r_rows = jnp.pad(r.reshape(L, B, H, nc).transpose(1, 3, 2, 0), ((0, 0), (0, 0), (0, 0), (0, 2 * HEAD_DIM - L)))
    d_rows = jnp.broadcast_to(decay[0].reshape(B, H, nc).transpose(0, 2, 1)[..., None], (B, nc, H, 2 * HEAD_DIM))
    rs = jnp.concatenate([r_rows, d_rows], axis=2)
    ya = _mlstm(z, cs, rs, conv_w, conv_b[None, :], mlstm_norm_g[None, :], B, S, t["nck"])

    bs_b = jnp.broadcast_to(gmlp_b_s[:, :, None], (GROUPS, GMLP_CHUNK, GROUP_DIM))
    yb = _gmlp(z, gmlp_ln_g[None, :], gmlp_ln_b[None, :], gmlp_w_s, bs_b, t["tg"])

    w_o = w_out.astype(BF16)
    x1, xt = _outproj(x2d, ya, yb, w_o[:W], w_o[W:], norm2_g[None, :], t["tm_out"])

    wq_t = peer_w_query.T.astype(BF16)
    keys = peer_sub_keys.reshape(2 * PEER_HEADS, PEER_KEYS, PEER_HALF).astype(BF16)
    r2, e2, n1, c1 = _topk(xt, wq_t, keys, t["tm_topk"])
    return _peer(xt, peer_u.astype(BF16), peer_v.T.astype(BF16), r2, e2, n1, c1, x1,
                 out_g[None, :], t["tm_peer"], t["te"])


def kernel(x, norm1_g, w_in, conv_w, conv_b, b_igate, b_fgate, mlstm_norm_g, gmlp_ln_g, gmlp_ln_b,
           gmlp_w_s, gmlp_b_s, w_out, norm2_g, peer_w_query, peer_sub_keys, peer_u, peer_v, final_g):
    B, S, D = x.shape
    depth = norm1_g.shape[0]
    assert depth == 1, "final rmsnorm is fused into the last layer's PEER kernel"
    x2d = x.reshape(B * S, D)
    out = _layer(x2d, B, S, norm1_g[0], w_in[0], conv_w[0], conv_b[0], b_igate[0], b_fgate[0],
                 mlstm_norm_g[0], gmlp_ln_g[0], gmlp_ln_b[0], gmlp_w_s[0], gmlp_b_s[0], w_out[0],
                 norm2_g[0], peer_w_query[0], peer_sub_keys[0], peer_u[0], peer_v[0], final_g)
    return out.reshape(B, S, D)
```

```python
import functools
import math

import jax
import jax.numpy as jnp
from jax import lax
from jax.experimental import pallas as pl
from jax.experimental.pallas import tpu as pltpu

F32 = jnp.float32
BF16 = jnp.bfloat16
EPS = 1e-6

CHUNK = 64
HEADS = 4
HEAD_DIM = 128
MLSTM_W = HEADS * HEAD_DIM
CONV_K = 4
GROUPS = 4
GROUP_DIM = 128
GMLP_W = GROUPS * GROUP_DIM
GMLP_CHUNK = 128
PEER_HEADS = 8
PEER_KEYS = 128
PEER_HALF = 128
TOPK = 16
NO_RANK = 127.0
LANES = 128
OUT_CHUNK = 256
GELU_C0 = math.sqrt(2.0 / math.pi)
GELU_C1 = GELU_C0 * 0.044715

VMEM_LIMIT = 56 * 1024 * 1024


def _params(sem):
    return pltpu.CompilerParams(dimension_semantics=sem, vmem_limit_bytes=VMEM_LIMIT)


def _rms(x, g):
    return x * lax.rsqrt(jnp.mean(x * x, axis=-1, keepdims=True) + EPS) * g


def _inproj_kernel(x_ref, g_ref, w_ref, z_ref, xn_ref):
    @pl.when(pl.program_id(1) == 0)
    def _():
        xn_ref[...] = _rms(x_ref[...], g_ref[...]).astype(BF16)

    z_ref[...] = jnp.dot(xn_ref[...], w_ref[...], preferred_element_type=F32)


def _inproj(x, g, w, tm, tn):
    T, D = x.shape
    N = w.shape[1]
    return pl.pallas_call(
        _inproj_kernel,
        grid=(T // tm, N // tn),
        in_specs=[pl.BlockSpec((tm, D), lambda i, j: (i, 0)),
                  pl.BlockSpec((1, D), lambda i, j: (0, 0)),
                  pl.BlockSpec((D, tn), lambda i, j: (0, j))],
        out_specs=pl.BlockSpec((tm, tn), lambda i, j: (i, j)),
        out_shape=jax.ShapeDtypeStruct((T, N), F32),
        scratch_shapes=[pltpu.VMEM((tm, D), BF16)],
        compiler_params=_params(("parallel", "arbitrary")),
        name="inproj",
    )(x, g, w)


def _gate_kernel(i_ref, f_ref, bi_ref, bf_ref,
                 cola_ref, sc_ref, emt_ref, wg_ref, r_ref, decay_ref, *, nchunks):
    ig = i_ref[...] + bi_ref[...]
    lf = jax.nn.log_sigmoid(f_ref[...] + bf_ref[...])
    L, NC = ig.shape
    row = lax.broadcasted_iota(jnp.int32, (L, NC), 0)

    b = lf
    sh = 1
    while sh < L:
        b = b + jnp.where(row >= sh, pltpu.roll(b, sh, axis=0), 0.0)
        sh *= 2
    r = ig - b
    cm = r
    sh = 1
    while sh < L:
        cm = jnp.maximum(cm, jnp.where(row >= sh, pltpu.roll(cm, sh, axis=0), -jnp.inf))
        sh *= 2

    bl = jnp.broadcast_to(b[L - 1:L, :], (8, NC))
    gmax = bl + jnp.broadcast_to(cm[L - 1:L, :], (8, NC))
    lane = lax.broadcasted_iota(jnp.int32, (8, NC), 1)
    first = (lane % nchunks) == 0

    def step(_, m_new):
        m_prev = jnp.where(first, 0.0, pltpu.roll(m_new, 1, axis=1))
        return jnp.maximum(bl + m_prev, gmax)

    m_new = lax.fori_loop(0, nchunks, step, gmax)
    m_prev = jnp.where(first, 0.0, pltpu.roll(m_new, 1, axis=1))

    inter = b + m_prev[0:1, :]
    m_t = jnp.maximum(inter, b + cm)
    cola_ref[...] = b - m_t
    sc_ref[...] = jnp.exp(inter - m_t)
    emt_ref[...] = jnp.exp(-m_t)
    wg_ref[...] = jnp.exp(bl[0:1, :] + r - m_new[0:1, :])
    r_ref[...] = r
    decay_ref[...] = jnp.exp(bl + m_prev - m_new)


def _gates(icol, fcol, bi_row, bf_row, nchunks):
    L, NC = icol.shape
    full = pl.BlockSpec((L, NC), lambda: (0, 0))
    row = pl.BlockSpec((1, NC), lambda: (0, 0))
    row8 = pl.BlockSpec((8, NC), lambda: (0, 0))
    return pl.pallas_call(
        functools.partial(_gate_kernel, nchunks=nchunks),
        in_specs=[full, full, row, row],
        out_specs=[full] * 5 + [row8],
        out_shape=[jax.ShapeDtypeStruct((L, NC), F32)] * 5 + [jax.ShapeDtypeStruct((8, NC), F32)],
        name="mlstm_gates",
    )(icol, fcol, bi_row, bf_row)


def _mlstm_kernel(q_ref, k_ref, v_ref, o_ref, cs_ref, rs_ref, cw_ref, cb_ref, ng_ref, y_ref,
                  caug_ref, hq_ref, hk_ref, *, nck):
    L, H, Dh, W = CHUNK, HEADS, HEAD_DIM, MLSTM_W
    sblk = nck * L

    @pl.when(pl.program_id(1) == 0)
    def _():
        caug_ref[...] = jnp.zeros_like(caug_ref)
        hq_ref[...] = jnp.zeros_like(hq_ref)
        hk_ref[...] = jnp.zeros_like(hk_ref)

    causal = (lax.broadcasted_iota(jnp.int32, (L, L), 0) >= lax.broadcasted_iota(jnp.int32, (L, L), 1))
    lane0 = lax.broadcasted_iota(jnp.int32, (L, Dh), 1) == 0

    def conv_silu(x_ref, hist_ref, c, cw, cb):
        xc = x_ref[pl.ds(pl.multiple_of(c * L, L), L), :]
        start = pl.multiple_of(jnp.maximum(c * L - 8, 0), 8)
        prev = jnp.where(c == 0, hist_ref[...], x_ref[pl.ds(start, 8), :])
        ext = jnp.concatenate([prev, xc], axis=0)
        out = cb
        for j in range(CONV_K):
            sh = CONV_K - 1 - j
            xs = ext if sh == 0 else pltpu.roll(ext, sh, axis=0)
            out = out + xs[8:, :] * cw[j:j + 1, :]
        return out * jax.nn.sigmoid(out)

    def body(c, carry):
        rows = pl.ds(pl.multiple_of(c * L, L), L)
        q = conv_silu(q_ref, hq_ref, c, cw_ref[:, :W], cb_ref[:, :W])
        k = conv_silu(k_ref, hk_ref, c, cw_ref[:, W:], cb_ref[:, W:]) * (Dh ** -0.5)
        v = v_ref[rows, :]
        kt = k.T
        cs = cs_ref[c]
        rs = rs_ref[c]
        outs = []
        for h in range(H):
            sl = slice(h * Dh, (h + 1) * Dh)
            qh = q[:, sl].astype(BF16)
            kh = k[:, sl].astype(BF16)
            kth = kt[sl, :].astype(BF16)
            vh = v[:, sl]
            cola = cs[:, h:h + 1]
            sc = cs[:, H + h:H + h + 1]
            emt = cs[:, 2 * H + h:2 * H + h + 1]
            wg = cs[:, 3 * H + h:3 * H + h + 1]
            rowr = rs[h:h + 1, :L]
            decay = rs[H + h:H + h + 1, :]
            sqk = lax.dot_general(qh, kh, (((1,), (1,)), ((), ())), preferred_element_type=F32)
            w = jnp.exp(jnp.where(causal, cola + rowr, -jnp.inf))
            smat = sqk * w
            den_i = jnp.sum(smat, axis=-1, keepdims=True)
            caug = caug_ref[h]
            qc = jnp.dot(qh, caug.astype(BF16), preferred_element_type=F32)
            num = jnp.dot(smat.astype(BF16), vh.astype(BF16), preferred_element_type=F32) + sc * qc[:, :Dh]
            den = den_i + sc * qc[:, Dh:Dh + 1]
            hh = num / jnp.maximum(jnp.abs(den), emt)
            outs.append(hh * lax.rsqrt(jnp.mean(hh * hh, axis=-1, keepdims=True) + EPS))
            vaug = jnp.concatenate([wg * vh, jnp.where(lane0, wg, 0.0)], axis=1).astype(BF16)
            dc = jnp.dot(kth, vaug, preferred_element_type=F32)
            caug_ref[h] = decay * caug + dc
        hcat = jnp.concatenate(outs, axis=1)
        y = hcat * ng_ref[...] * jax.nn.sigmoid(o_ref[rows, :])
        y_ref[rows, :] = y.astype(y_ref.dtype)
        return carry

    lax.fori_loop(0, nck, body, 0)
    hq_ref[...] = q_ref[pl.ds(sblk - 8, 8), :]
    hk_ref[...] = k_ref[pl.ds(sblk - 8, 8), :]


def _mlstm(z, cs, rs, conv_w, conv_b, norm_g, B, S, nck):
    T = z.shape[0]
    L, H, Dh, W = CHUNK, HEADS, HEAD_DIM, MLSTM_W
    sblk = nck * L
    nsb = S // sblk

    def zspec(col):
        return pl.BlockSpec((sblk, W), lambda b, s, col=col: (b * nsb + s, col))

    return pl.pallas_call(
        functools.partial(_mlstm_kernel, nck=nck),
        grid=(B, nsb),
        in_specs=[zspec(0), zspec(1), zspec(2), zspec(3),
                  pl.BlockSpec((None, nck, L, 4 * H), lambda b, s: (b, s, 0, 0)),
                  pl.BlockSpec((None, nck, 2 * H, 2 * Dh), lambda b, s: (b, s, 0, 0)),
                  pl.BlockSpec((CONV_K, 2 * W), lambda b, s: (0, 0)),
                  pl.BlockSpec((1, 2 * W), lambda b, s: (0, 0)),
                  pl.BlockSpec((1, W), lambda b, s: (0, 0))],
        out_specs=pl.BlockSpec((sblk, W), lambda b, s: (b * nsb + s, 0)),
        out_shape=jax.ShapeDtypeStruct((T, W), BF16),
        scratch_shapes=[pltpu.VMEM((H, Dh, 2 * Dh), F32),
                        pltpu.VMEM((8, W), F32),
                        pltpu.VMEM((8, W), F32)],
        compiler_params=_params(("parallel", "arbitrary")),
        name="mlstm",
    )(z, z, z, z, cs, rs, conv_w, conv_b, norm_g)


def _gmlp_kernel(u_ref, v_ref, lg_ref, lb_ref, ws_ref, bs_ref, y_ref, *, nchunk):
    C, G, Dg = GMLP_CHUNK, GROUPS, GROUP_DIM
    pos_t = lax.broadcasted_iota(jnp.int32, (C, C), 0) // CHUNK
    pos_s = lax.broadcasted_iota(jnp.int32, (C, C), 1) // CHUNK
    mask = (pos_t >= pos_s).astype(F32)
    for g in range(G):
        wm = (ws_ref[g] * mask).astype(BF16)
        cols = slice(g * Dg, (g + 1) * Dg)
        for c in range(nchunk):
            rows = slice(c * C, (c + 1) * C)
            vv = jax.nn.gelu(v_ref[rows, cols])
            mu = jnp.mean(vv, axis=-1, keepdims=True)
            var = jnp.mean(jnp.square(vv - mu), axis=-1, keepdims=True)
            vv = (vv - mu) * lax.rsqrt(var + EPS)
            vv = vv * lg_ref[:, cols] + lb_ref[:, cols]
            mix = jnp.dot(wm, vv.astype(BF16), preferred_element_type=F32) + bs_ref[g]
            y_ref[rows, cols] = (jax.nn.gelu(u_ref[rows, cols]) * mix).astype(y_ref.dtype)


def _gmlp(z, ln_g, ln_b, w_s, bs_b, tg):
    T = z.shape[0]
    W, C, G = GMLP_W, GMLP_CHUNK, GROUPS
    return pl.pallas_call(
        functools.partial(_gmlp_kernel, nchunk=tg // C),
        grid=(T // tg,),
        in_specs=[pl.BlockSpec((tg, W), lambda i: (i, 4)),
                  pl.BlockSpec((tg, W), lambda i: (i, 5)),
                  pl.BlockSpec((1, W), lambda i: (0, 0)),
                  pl.BlockSpec((1, W), lambda i: (0, 0)),
                  pl.BlockSpec((G, C, C), lambda i: (0, 0, 0)),
                  pl.BlockSpec((G, C, GROUP_DIM), lambda i: (0, 0, 0))],
        out_specs=pl.BlockSpec((tg, W), lambda i: (i, 0)),
        out_shape=jax.ShapeDtypeStruct((T, W), BF16),
        compiler_params=_params(("parallel",)),
        name="gmlp",
    )(z, z, ln_g, ln_b, w_s, bs_b)


def _outproj_kernel(x_ref, ya_ref, yb_ref, wa_ref, wb_ref, g_ref, x1_ref, xt_ref):
    x1 = (x_ref[...]
          + jnp.dot(ya_ref[...], wa_ref[...], preferred_element_type=F32)
          + jnp.dot(yb_ref[...], wb_ref[...], preferred_element_type=F32))
    x1_ref[...] = x1
    xt_ref[...] = _rms(x1, g_ref[...]).T.astype(BF16)


def _outproj(x, ya, yb, wa, wb, g, tm):
    T, D = x.shape
    Wh = ya.shape[1]
    return pl.pallas_call(
        _outproj_kernel,
        grid=(T // tm,),
        in_specs=[pl.BlockSpec((tm, D), lambda i: (i, 0)),
                  pl.BlockSpec((tm, Wh), lambda i: (i, 0)),
                  pl.BlockSpec((tm, Wh), lambda i: (i, 0)),
                  pl.BlockSpec((Wh, D), lambda i: (0, 0)),
                  pl.BlockSpec((Wh, D), lambda i: (0, 0)),
                  pl.BlockSpec((1, D), lambda i: (0, 0))],
        out_specs=[pl.BlockSpec((tm, D), lambda i: (i, 0)),
                   pl.BlockSpec((D, tm), lambda i: (0, i))],
        out_shape=[jax.ShapeDtypeStruct((T, D), F32), jax.ShapeDtypeStruct((D, T), BF16)],
        compiler_params=_params(("parallel",)),
        name="outproj",
    )(x, ya, yb, wa, wb, g)


def _topk_kernel(xt_ref, wq_ref, keys_ref, r2_ref, e2_ref, n1_ref, c1_ref,
                 qt_ref, s_ref, a_ref, b_ref, cnt_ref, iz_ref, r1_ref, *, tm):
    NH, K = PEER_HEADS, PEER_KEYS
    qt_ref[...] = jnp.dot(wq_ref[...], xt_ref[...], preferred_element_type=F32).astype(BF16)
    for hp in range(2 * NH):
        s_ref[hp] = jnp.dot(keys_ref[hp], qt_ref[hp * PEER_HALF:(hp + 1) * PEER_HALF, :],
                            preferred_element_type=F32)

    pairs = [(i, j) for i in range(TOPK) for j in range(TOPK) if (i + 1) * (j + 1) <= TOPK]
    key_id = lax.broadcasted_iota(jnp.int32, (K, LANES), 0).astype(F32)

    def select_stats(lanes, tie_break):
        bad = jnp.zeros((1, LANES), F32)
        for h in range(NH):
            for p in range(2):
                cur = s_ref[2 * h + p, :, lanes]
                rank = jnp.full(cur.shape, NO_RANK, F32)
                want_rank = p == 1 or tie_break
                for j in range(TOPK):
                    m = jnp.max(cur, axis=0, keepdims=True)
                    hit = cur == m
                    if tie_break:
                        first = jnp.min(jnp.where(hit, key_id, float(K)), axis=0, keepdims=True)
                        hit = key_id == first
                    (a_ref if p == 0 else b_ref)[j, h:h + 1, :] = m
                    if want_rank:
                        rank = jnp.where(hit, float(j), rank)
                    cur = jnp.where(hit, -jnp.inf, cur)
                if not tie_break:
                    taken = jnp.sum(jnp.where(cur == -jnp.inf, 1.0, 0.0), axis=0, keepdims=True)
                    bad = jnp.maximum(bad, jnp.where(taken != float(TOPK), 1.0, 0.0))
                if p == 1:
                    r2_ref[h, :, lanes] = rank.astype(r2_ref.dtype)
                elif tie_break:
                    r1_ref[h] = rank
        a = [a_ref[i] for i in range(TOPK)]
        b = [b_ref[j] for j in range(TOPK)]
        cand = [a[i] + b[j] for (i, j) in pairs]
        if tie_break:
            sel = []
            for ip, cp in enumerate(cand):
                ahead = jnp.zeros(cp.shape, F32)
                for iq, cq in enumerate(cand):
                    if iq != ip:
                        ahead = ahead + jnp.where((cq >= cp) if iq < ip else (cq > cp), 1.0, 0.0)
                sel.append(ahead < float(TOPK))
        else:
            cur = list(cand)
            for j in range(TOPK):
                tau = functools.reduce(jnp.maximum, cur)
                cur = [jnp.where(c == tau, -jnp.inf, c) for c in cur]
            sel = [c >= tau for c in cand]
        zsum = jnp.zeros(cand[0].shape, F32)
        cnt = [jnp.zeros(cand[0].shape, F32) for _ in range(TOPK)]
        for (i, j), cp, sp in zip(pairs, cand, sel):
            cnt[i] = cnt[i] + jnp.where(sp, 1.0, 0.0)
            zsum = zsum + jnp.where(sp, jnp.exp(cp - cand[0]), 0.0)
        for i in range(TOPK):
            cnt_ref[i] = cnt[i]
        iz_ref[...] = 0.5 / zsum
        if not tie_break:
            total = functools.reduce(jnp.add, cnt)
            bad8 = jnp.where(total != float(TOPK), 1.0, 0.0)
            bad = jnp.maximum(bad, jnp.max(bad8, axis=0, keepdims=True))
        for h in range(NH):
            s1 = s_ref[2 * h, :, lanes]
            n1 = jnp.zeros(s1.shape, F32)
            for i in range(TOPK):
                hit = (r1_ref[h] == float(i)) if tie_break else (s1 == a_ref[i, h:h + 1, :])
                n1 = n1 + jnp.where(hit, cnt_ref[i, h:h + 1, :], 0.0)
            n1_ref[h, :, lanes] = n1
            c1_ref[h, :, lanes] = jnp.exp(s1 - a_ref[0, h:h + 1, :]) * iz_ref[h:h + 1, :]
            s2 = s_ref[2 * h + 1, :, lanes]
            e2_ref[h, :, lanes] = jnp.exp(s2 - b_ref[0, h:h + 1, :]).astype(e2_ref.dtype)
        return bad

    def group(gi, carry):
        lanes = pl.ds(pl.multiple_of(gi * LANES, LANES), LANES)
        bad = select_stats(lanes, tie_break=False)

        @pl.when(jnp.max(bad) > 0.0)
        def _():
            select_stats(lanes, tie_break=True)

        return carry

    lax.fori_loop(0, tm // LANES, group, 0)


def _topk(xt, wq_t, keys, tm):
    D, T = xt.shape
    NH, K = PEER_HEADS, PEER_KEYS
    stat = pl.BlockSpec((NH, K, tm), lambda i: (0, 0, i))
    return pl.pallas_call(
        functools.partial(_topk_kernel, tm=tm),
        grid=(T // tm,),
        in_specs=[pl.BlockSpec((D, tm), lambda i: (0, i)),
                  pl.BlockSpec(wq_t.shape, lambda i: (0, 0)),
                  pl.BlockSpec(keys.shape, lambda i: (0, 0, 0))],
        out_specs=[stat] * 4,
        out_shape=[jax.ShapeDtypeStruct((NH, K, T), BF16), jax.ShapeDtypeStruct((NH, K, T), BF16),
                   jax.ShapeDtypeStruct((NH, K, T), F32), jax.ShapeDtypeStruct((NH, K, T), F32)],
        scratch_shapes=[pltpu.VMEM((wq_t.shape[0], tm), BF16),
                        pltpu.VMEM((2 * NH, K, tm), F32),
                        pltpu.VMEM((TOPK, NH, LANES), F32),
                        pltpu.VMEM((TOPK, NH, LANES), F32),
                        pltpu.VMEM((TOPK, NH, LANES), F32),
                        pltpu.VMEM((NH, LANES), F32),
                        pltpu.VMEM((NH, K, LANES), F32)],
        compiler_params=_params(("parallel",)),
        name="peer_topk",
    )(xt, wq_t, keys)


def _gelu2(x):
    return x + x * jnp.tanh(x * (GELU_C0 + GELU_C1 * (x * x)))


def _peer_kernel(xt_ref, u_ref, vt_ref, r2_ref, e2_ref, n1_ref, c1_ref, x1_ref, fg_ref, o_ref,
                 acc_ref, pt_ref, at_ref, *, te, tm, nsplit):
    NH, K = PEER_HEADS, PEER_KEYS
    e = pl.program_id(1)

    @pl.when(e == 0)
    def _():
        acc_ref[...] = jnp.zeros_like(acc_ref)

    n_i1 = te // K
    i1_rows = pl.ds(pl.multiple_of(e * n_i1, n_i1), n_i1)
    sub = te // nsplit

    def scores(s):
        at_ref[s] = jnp.dot(u_ref[s * sub:(s + 1) * sub, :], xt_ref[...], preferred_element_type=F32)

    def gated(s):
        nk = sub // K
        for lg in range(tm // LANES):
            lanes = slice(lg * LANES, (lg + 1) * LANES)
            gates = [None] * nk
            for h in range(NH):
                r2 = r2_ref[h, :, lanes]
                e2 = e2_ref[h, :, lanes]
                n1t = n1_ref[h, i1_rows, lanes]
                c1t = c1_ref[h, i1_rows, lanes]
                for k in range(nk):
                    kk = s * nk + k
                    n1 = jnp.broadcast_to(n1t[kk:kk + 1, :], (K, LANES)).astype(BF16)
                    c1 = jnp.broadcast_to(c1t[kk:kk + 1, :], (K, LANES)).astype(BF16)
                    term = jnp.minimum(jnp.maximum(n1 - r2, 0), e2) * c1
                    gates[k] = term if gates[k] is None else gates[k] + term
            for k in range(nk):
                kk = s * nk + k
                act = _gelu2(at_ref[s, k * K:(k + 1) * K, lanes]).astype(BF16)
                pt_ref[kk * K:(kk + 1) * K, lanes] = act * gates[k]

    def mixed(s):
        rows = slice(s * sub, (s + 1) * sub)
        for dm in range(0, acc_ref.shape[0], OUT_CHUNK):
            out = slice(dm, dm + OUT_CHUNK)
            acc_ref[out, :] += jnp.dot(vt_ref[out, rows], pt_ref[rows, :], preferred_element_type=F32)

    for s in range(nsplit):
        scores(s)
    for s in range(nsplit):
        gated(s)
        mixed(s)

    @pl.when(e == pl.num_programs(1) - 1)
    def _():
        o_ref[...] = _rms(x1_ref[...] + acc_ref[...].T, fg_ref[...])


def _peer(xt, u, vt, r2, e2, n1, c1, x1, fg, tm, te, nsplit=4):
    D, T = xt.shape
    E = u.shape[0]
    NH, K = PEER_HEADS, PEER_KEYS
    assert te == 8 * K, "a step must cover exactly one sublane tile of first-half keys"
    stat = pl.BlockSpec((NH, K, tm), lambda i, e: (0, 0, i))
    return pl.pallas_call(
        functools.partial(_peer_kernel, te=te, tm=tm, nsplit=nsplit),
        grid=(T // tm, E // te),
        in_specs=[pl.BlockSpec((D, tm), lambda i, e: (0, i)),
                  pl.BlockSpec((te, D), lambda i, e: (e, 0)),
                  pl.BlockSpec((D, te), lambda i, e: (0, e)),
                  stat, stat, stat, stat,
                  pl.BlockSpec((tm, D), lambda i, e: (i, 0)),
                  pl.BlockSpec((1, D), lambda i, e: (0, 0))],
        out_specs=pl.BlockSpec((tm, D), lambda i, e: (i, 0)),
        out_shape=jax.ShapeDtypeStruct((T, D), F32),
        scratch_shapes=[pltpu.VMEM((D, tm), F32), pltpu.VMEM((te, tm), BF16),
                        pltpu.VMEM((nsplit, te // nsplit, tm), F32)],
        compiler_params=_params(("parallel", "arbitrary")),
        name="peer_dense",
    )(xt, u, vt, r2, e2, n1, c1, x1, fg)


def _tiles(T, S):
    def pick(n, pref):
        t = min(pref, n)
        while n % t:
            t //= 2
        return t
    return dict(tm_in=pick(T, 512), tn_in=640, nck=pick(S // CHUNK, 16), tg=pick(T, 512),
                tm_out=pick(T, 512), tm_topk=pick(T, 512), tm_peer=pick(T, 512), te=8 * PEER_KEYS)


def _layer(x2d, B, S, norm1_g, w_in, conv_w, conv_b, b_igate, b_fgate, mlstm_norm_g,
           gmlp_ln_g, gmlp_ln_b, gmlp_w_s, gmlp_b_s, w_out, norm2_g,
           peer_w_query, peer_sub_keys, peer_u, peer_v, out_g):
    T, D = x2d.shape
    H, L, W = HEADS, CHUNK, MLSTM_W
    nc = S // L
    t = _tiles(T, S)

    g0 = 4 * W
    u0 = g0 + 2 * H
    w_r = jnp.concatenate([w_in[:, :g0], w_in[:, u0:], w_in[:, g0:u0],
                           jnp.zeros((D, 128 - 2 * H), w_in.dtype)], axis=1).astype(BF16)
    z = _inproj(x2d, norm1_g[None, :], w_r, t["tm_in"], t["tn_in"])

    gcols = z[:, 6 * W:6 * W + 2 * H].reshape(B, nc, L, 2, H).transpose(3, 2, 0, 4, 1).reshape(2, L, B * H * nc)
    bi_row = jnp.broadcast_to(b_igate[None, :, None], (B, H, nc)).reshape(1, -1)
    bf_row = jnp.broadcast_to(b_fgate[None, :, None], (B, H, nc)).reshape(1, -1)
    cola, sc, emt, wg, r, decay = _gates(gcols[0], gcols[1], bi_row, bf_row, nc)
    cs = jnp.stack([cola, sc, emt, wg]).reshape(4, L, B, H, nc).transpose(2, 4, 1, 0, 3).reshape(B, nc, L, 4 * H)
    r_rows = jnp.pad(r.reshape(L, B, H, nc).transpose(1, 3, 2, 0), ((0, 0), (0, 0), (0, 0), (0, 2 * HEAD_DIM - L)))
    d_rows = jnp.broadcast_to(decay[0].reshape(B, H, nc).transpose(0, 2, 1)[..., None], (B, nc, H, 2 * HEAD_DIM))
    rs = jnp.concatenate([r_rows, d_rows], axis=2)
    ya = _mlstm(z, cs, rs, conv_w, conv_b[None, :], mlstm_norm_g[None, :], B, S, t["nck"])

    bs_b = jnp.broadcast_to(gmlp_b_s[:, :, None], (GROUPS, GMLP_CHUNK, GROUP_DIM))
    yb = _gmlp(z, gmlp_ln_g[None, :], gmlp_ln_b[None, :], gmlp_w_s, bs_b, t["tg"])

    w_o = w_out.astype(BF16)
    x1, xt = _outproj(x2d, ya, yb, w_o[:W], w_o[W:], norm2_g[None, :], t["tm_out"])

    wq_t = peer_w_query.T.astype(BF16)
    keys = peer_sub_keys.reshape(2 * PEER_HEADS, PEER_KEYS, PEER_HALF).astype(BF16)
    r2, e2, n1, c1 = _topk(xt, wq_t, keys, t["tm_topk"])
    return _peer(xt, peer_u.astype(BF16), peer_v.T.astype(BF16), r2, e2, n1, c1, x1,
                 out_g[None, :], t["tm_peer"], t["te"])


def kernel(x, norm1_g, w_in, conv_w, conv_b, b_igate, b_fgate, mlstm_norm_g, gmlp_ln_g, gmlp_ln_b,
           gmlp_w_s, gmlp_b_s, w_out, norm2_g, peer_w_query, peer_sub_keys, peer_u, peer_v, final_g):
    B, S, D = x.shape
    depth = norm1_g.shape[0]
    assert depth == 1, "final rmsnorm is fused into the last layer's PEER kernel"
    x2d = x.reshape(B * S, D)
    out = _layer(x2d, B, S, norm1_g[0], w_in[0], conv_w[0], conv_b[0], b_igate[0], b_fgate[0],
                 mlstm_norm_g[0], gmlp_ln_g[0], gmlp_ln_b[0], gmlp_w_s[0], gmlp_b_s[0], w_out[0],
                 norm2_g[0], peer_w_query[0], peer_sub_keys[0], peer_u[0], peer_v[0], final_g)
    return out.reshape(B, S, D)
```

```python
import functools
import math

import jax
import jax.numpy as jnp
from jax import lax
from jax.experimental import pallas as pl
from jax.experimental.pallas import tpu as pltpu

F32 = jnp.float32
BF16 = jnp.bfloat16
EPS = 1e-6

CHUNK = 64
HEADS = 4
HEAD_DIM = 128
MLSTM_W = HEADS * HEAD_DIM
CONV_K = 4
GROUPS = 4
GROUP_DIM = 128
GMLP_W = GROUPS * GROUP_DIM
GMLP_CHUNK = 128
PEER_HEADS = 8
PEER_KEYS = 128
PEER_HALF = 128
TOPK = 16
NO_RANK = 127.0
LANES = 128
BF16_ROWS = 16
OUT_CHUNK = 256
GELU_C0 = math.sqrt(2.0 / math.pi)
GELU_C1 = GELU_C0 * 0.044715

VMEM_LIMIT = 56 * 1024 * 1024


def _params(sem):
    return pltpu.CompilerParams(dimension_semantics=sem, vmem_limit_bytes=VMEM_LIMIT)


def _rms(x, g):
    return x * lax.rsqrt(jnp.mean(x * x, axis=-1, keepdims=True) + EPS) * g


def _inproj_kernel(x_ref, g_ref, w_ref, wg_ref, z_ref, gate_ref, xn_ref):
    j = pl.program_id(1)

    @pl.when(j == 0)
    def _():
        xn_ref[...] = _rms(x_ref[...], g_ref[...]).astype(BF16)

    z_ref[...] = jnp.dot(xn_ref[...], w_ref[...], preferred_element_type=F32).astype(z_ref.dtype)

    @pl.when(j == pl.num_programs(1) - 1)
    def _():
        gate_ref[...] = jnp.dot(xn_ref[...], wg_ref[...], preferred_element_type=F32)


def _inproj(x, g, w, wg, tm, tn):
    T, D = x.shape
    N = w.shape[1]
    return pl.pallas_call(
        _inproj_kernel,
        grid=(T // tm, N // tn),
        in_specs=[pl.BlockSpec((tm, D), lambda i, j: (i, 0)),
                  pl.BlockSpec((1, D), lambda i, j: (0, 0)),
                  pl.BlockSpec((D, tn), lambda i, j: (0, j)),
                  pl.BlockSpec(wg.shape, lambda i, j: (0, 0))],
        out_specs=[pl.BlockSpec((tm, tn), lambda i, j: (i, j)),
                   pl.BlockSpec((tm, wg.shape[1]), lambda i, j: (i, 0))],
        out_shape=[jax.ShapeDtypeStruct((T, N), BF16), jax.ShapeDtypeStruct((T, wg.shape[1]), F32)],
        scratch_shapes=[pltpu.VMEM((tm, D), BF16)],
        compiler_params=_params(("parallel", "arbitrary")),
        name="inproj",
    )(x, g, w, wg)


def _gate_kernel(i_ref, f_ref, bi_ref, bf_ref,
                 cola_ref, sc_ref, emt_ref, wg_ref, r_ref, decay_ref, *, nchunks):
    ig = i_ref[...] + bi_ref[...]
    lf = jax.nn.log_sigmoid(f_ref[...] + bf_ref[...])
    L, NC = ig.shape
    row = lax.broadcasted_iota(jnp.int32, (L, NC), 0)

    b = lf
    sh = 1
    while sh < L:
        b = b + jnp.where(row >= sh, pltpu.roll(b, sh, axis=0), 0.0)
        sh *= 2
    r = ig - b
    cm = r
    sh = 1
    while sh < L:
        cm = jnp.maximum(cm, jnp.where(row >= sh, pltpu.roll(cm, sh, axis=0), -jnp.inf))
        sh *= 2

    bl = jnp.broadcast_to(b[L - 1:L, :], (8, NC))
    gmax = bl + jnp.broadcast_to(cm[L - 1:L, :], (8, NC))
    lane = lax.broadcasted_iota(jnp.int32, (8, NC), 1)
    first = (lane % nchunks) == 0

    def step(_, m_new):
        m_prev = jnp.where(first, 0.0, pltpu.roll(m_new, 1, axis=1))
        return jnp.maximum(bl + m_prev, gmax)

    m_new = lax.fori_loop(0, nchunks, step, gmax)
    m_prev = jnp.where(first, 0.0, pltpu.roll(m_new, 1, axis=1))

    inter = b + m_prev[0:1, :]
    m_t = jnp.maximum(inter, b + cm)
    cola_ref[...] = b - m_t
    sc_ref[...] = jnp.exp(inter - m_t)
    emt_ref[...] = jnp.exp(-m_t)
    wg_ref[...] = jnp.exp(bl[0:1, :] + r - m_new[0:1, :])
    r_ref[...] = r
    decay_ref[...] = jnp.exp(bl + m_prev - m_new)


def _gates(icol, fcol, bi_row, bf_row, nchunks):
    L, NC = icol.shape
    full = pl.BlockSpec((L, NC), lambda: (0, 0))
    row = pl.BlockSpec((1, NC), lambda: (0, 0))
    row8 = pl.BlockSpec((8, NC), lambda: (0, 0))
    return pl.pallas_call(
        functools.partial(_gate_kernel, nchunks=nchunks),
        in_specs=[full, full, row, row],
        out_specs=[full] * 5 + [row8],
        out_shape=[jax.ShapeDtypeStruct((L, NC), F32)] * 5 + [jax.ShapeDtypeStruct((8, NC), F32)],
        name="mlstm_gates",
    )(icol, fcol, bi_row, bf_row)


def _mlstm_kernel(q_ref, k_ref, v_ref, o_ref, cs_ref, rs_ref, cw_ref, cb_ref, ng_ref, y_ref,
                  caug_ref, hq_ref, hk_ref, *, nck):
    L, H, Dh, W = CHUNK, HEADS, HEAD_DIM, MLSTM_W
    sblk = nck * L

    @pl.when(pl.program_id(1) == 0)
    def _():
        caug_ref[...] = jnp.zeros_like(caug_ref)
        hq_ref[...] = jnp.zeros_like(hq_ref)
        hk_ref[...] = jnp.zeros_like(hk_ref)

    causal = (lax.broadcasted_iota(jnp.int32, (L, L), 0) >= lax.broadcasted_iota(jnp.int32, (L, L), 1))
    lane0 = lax.broadcasted_iota(jnp.int32, (L, Dh), 1) == 0

    def conv_silu(x_ref, hist_ref, c, cw, cb):
        xc = x_ref[pl.ds(pl.multiple_of(c * L, L), L), :].astype(F32)
        start = pl.multiple_of(jnp.maximum(c * L - BF16_ROWS, 0), BF16_ROWS)
        prev = x_ref[pl.ds(start, BF16_ROWS), :].astype(F32)[BF16_ROWS - 8:, :]
        prev = jnp.where(c == 0, hist_ref[...], prev)
        ext = jnp.concatenate([prev, xc], axis=0)
        out = cb
        for j in range(CONV_K):
            sh = CONV_K - 1 - j
            xs = ext if sh == 0 else pltpu.roll(ext, sh, axis=0)
            out = out + xs[8:, :] * cw[j:j + 1, :]
        return out * jax.nn.sigmoid(out)

    def body(c, carry):
        rows = pl.ds(pl.multiple_of(c * L, L), L)
        q = conv_silu(q_ref, hq_ref, c, cw_ref[:, :W], cb_ref[:, :W])
        k = conv_silu(k_ref, hk_ref, c, cw_ref[:, W:], cb_ref[:, W:]) * (Dh ** -0.5)
        v = v_ref[rows, :].astype(F32)
        kt = k.T
        cs = cs_ref[c]
        rs = rs_ref[c]
        outs = []
        for h in range(H):
            sl = slice(h * Dh, (h + 1) * Dh)
            qh = q[:, sl].astype(BF16)
            kh = k[:, sl].astype(BF16)
            kth = kt[sl, :].astype(BF16)
            vh = v[:, sl]
            cola = cs[:, h:h + 1]
            sc = cs[:, H + h:H + h + 1]
            emt = cs[:, 2 * H + h:2 * H + h + 1]
            wg = cs[:, 3 * H + h:3 * H + h + 1]
            rowr = rs[h:h + 1, :L]
            decay = rs[H + h:H + h + 1, :]
            sqk = lax.dot_general(qh, kh, (((1,), (1,)), ((), ())), preferred_element_type=F32)
            w = jnp.exp(jnp.where(causal, cola + rowr, -jnp.inf))
            smat = sqk * w
            den_i = jnp.sum(smat, axis=-1, keepdims=True)
            caug = caug_ref[h]
            qc = jnp.dot(qh, caug.astype(BF16), preferred_element_type=F32)
            num = jnp.dot(smat.astype(BF16), vh.astype(BF16), preferred_element_type=F32) + sc * qc[:, :Dh]
            den = den_i + sc * qc[:, Dh:Dh + 1]
            hh = num / jnp.maximum(jnp.abs(den), emt)
            outs.append(hh * lax.rsqrt(jnp.mean(hh * hh, axis=-1, keepdims=True) + EPS))
            vaug = jnp.concatenate([wg * vh, jnp.where(lane0, wg, 0.0)], axis=1).astype(BF16)
            dc = jnp.dot(kth, vaug, preferred_element_type=F32)
            caug_ref[h] = decay * caug + dc
        hcat = jnp.concatenate(outs, axis=1)
        y = hcat * ng_ref[...] * jax.nn.sigmoid(o_ref[rows, :].astype(F32))
        y_ref[rows, :] = y.astype(y_ref.dtype)
        return carry

    lax.fori_loop(0, nck, body, 0)
    hq_ref[...] = q_ref[pl.ds(sblk - BF16_ROWS, BF16_ROWS), :].astype(F32)[BF16_ROWS - 8:, :]
    hk_ref[...] = k_ref[pl.ds(sblk - BF16_ROWS, BF16_ROWS), :].astype(F32)[BF16_ROWS - 8:, :]


def _mlstm(z, cs, rs, conv_w, conv_b, norm_g, B, S, nck):
    T = z.shape[0]
    L, H, Dh, W = CHUNK, HEADS, HEAD_DIM, MLSTM_W
    sblk = nck * L
    nsb = S // sblk

    def zspec(col):
        return pl.BlockSpec((sblk, W), lambda b, s, col=col: (b * nsb + s, col))

    return pl.pallas_call(
        functools.partial(_mlstm_kernel, nck=nck),
        grid=(B, nsb),
        in_specs=[zspec(0), zspec(1), zspec(2), zspec(3),
                  pl.BlockSpec((None, nck, L, 4 * H), lambda b, s: (b, s, 0, 0)),
                  pl.BlockSpec((None, nck, 2 * H, 2 * Dh), lambda b, s: (b, s, 0, 0)),
                  pl.BlockSpec((CONV_K, 2 * W), lambda b, s: (0, 0)),
                  pl.BlockSpec((1, 2 * W), lambda b, s: (0, 0)),
                  pl.BlockSpec((1, W), lambda b, s: (0, 0))],
        out_specs=pl.BlockSpec((sblk, W), lambda b, s: (b * nsb + s, 0)),
        out_shape=jax.ShapeDtypeStruct((T, W), BF16),
        scratch_shapes=[pltpu.VMEM((H, Dh, 2 * Dh), F32),
                        pltpu.VMEM((8, W), F32),
                        pltpu.VMEM((8, W), F32)],
        compiler_params=_params(("parallel", "arbitrary")),
        name="mlstm",
    )(z, z, z, z, cs, rs, conv_w, conv_b, norm_g)


def _gmlp_kernel(u_ref, v_ref, lg_ref, lb_ref, ws_ref, bs_ref, y_ref, *, nchunk):
    C, G, Dg = GMLP_CHUNK, GROUPS, GROUP_DIM
    pos_t = lax.broadcasted_iota(jnp.int32, (C, C), 0) // CHUNK
    pos_s = lax.broadcasted_iota(jnp.int32, (C, C), 1) // CHUNK
    mask = (pos_t >= pos_s).astype(F32)
    for g in range(G):
        wm = (ws_ref[g] * mask).astype(BF16)
        cols = slice(g * Dg, (g + 1) * Dg)
        for c in range(nchunk):
            rows = slice(c * C, (c + 1) * C)
            vv = jax.nn.gelu(v_ref[rows, cols].astype(F32))
            mu = jnp.mean(vv, axis=-1, keepdims=True)
            var = jnp.mean(jnp.square(vv - mu), axis=-1, keepdims=True)
            vv = (vv - mu) * lax.rsqrt(var + EPS)
            vv = vv * lg_ref[:, cols] + lb_ref[:, cols]
            mix = jnp.dot(wm, vv.astype(BF16), preferred_element_type=F32) + bs_ref[g]
            y_ref[rows, cols] = (jax.nn.gelu(u_ref[rows, cols].astype(F32)) * mix).astype(y_ref.dtype)


def _gmlp(z, ln_g, ln_b, w_s, bs_b, tg):
    T = z.shape[0]
    W, C, G = GMLP_W, GMLP_CHUNK, GROUPS
    return pl.pallas_call(
        functools.partial(_gmlp_kernel, nchunk=tg // C),
        grid=(T // tg,),
        in_specs=[pl.BlockSpec((tg, W), lambda i: (i, 4)),
                  pl.BlockSpec((tg, W), lambda i: (i, 5)),
                  pl.BlockSpec((1, W), lambda i: (0, 0)),
                  pl.BlockSpec((1, W), lambda i: (0, 0)),
                  pl.BlockSpec((G, C, C), lambda i: (0, 0, 0)),
                  pl.BlockSpec((G, C, GROUP_DIM), lambda i: (0, 0, 0))],
        out_specs=pl.BlockSpec((tg, W), lambda i: (i, 0)),
        out_shape=jax.ShapeDtypeStruct((T, W), BF16),
        compiler_params=_params(("parallel",)),
        name="gmlp",
    )(z, z, ln_g, ln_b, w_s, bs_b)


def _outproj_kernel(x_ref, ya_ref, yb_ref, wa_ref, wb_ref, g_ref, x1_ref, xt_ref):
    x1 = (x_ref[...]
          + jnp.dot(ya_ref[...], wa_ref[...], preferred_element_type=F32)
          + jnp.dot(yb_ref[...], wb_ref[...], preferred_element_type=F32))
    x1_ref[...] = x1
    xt_ref[...] = _rms(x1, g_ref[...]).T.astype(BF16)


def _outproj(x, ya, yb, wa, wb, g, tm):
    T, D = x.shape
    Wh = ya.shape[1]
    return pl.pallas_call(
        _outproj_kernel,
        grid=(T // tm,),
        in_specs=[pl.BlockSpec((tm, D), lambda i: (i, 0)),
                  pl.BlockSpec((tm, Wh), lambda i: (i, 0)),
                  pl.BlockSpec((tm, Wh), lambda i: (i, 0)),
                  pl.BlockSpec((Wh, D), lambda i: (0, 0)),
                  pl.BlockSpec((Wh, D), lambda i: (0, 0)),
                  pl.BlockSpec((1, D), lambda i: (0, 0))],
        out_specs=[pl.BlockSpec((tm, D), lambda i: (i, 0)),
                   pl.BlockSpec((D, tm), lambda i: (0, i))],
        out_shape=[jax.ShapeDtypeStruct((T, D), F32), jax.ShapeDtypeStruct((D, T), BF16)],
        compiler_params=_params(("parallel",)),
        name="outproj",
    )(x, ya, yb, wa, wb, g)


def _oddeven_mergesort(lo, hi):
    def merge(lo, hi, r):
        step = 2 * r
        if step < hi - lo:
            yield from merge(lo, hi, step)
            yield from merge(lo + r, hi, step)
            yield from ((i, i + r) for i in range(lo + r, hi - r, step))
        else:
            yield (lo, lo + r)
    if hi > lo:
        mid = lo + (hi - lo) // 2
        yield from _oddeven_mergesort(lo, mid)
        yield from _oddeven_mergesort(mid + 1, hi)
        yield from merge(lo, hi, 1)


SORT16 = tuple(_oddeven_mergesort(0, TOPK - 1))
BITONIC16 = tuple((i, i + d) for d in (8, 4, 2, 1) for i in range(TOPK) if not i & d)


def _exchange(v, net):
    v = list(v)
    for i, j in net:
        v[i], v[j] = jnp.maximum(v[i], v[j]), jnp.minimum(v[i], v[j])
    return v


def _top16_sorted(s):
    v = _exchange([s[8 * r:8 * r + 8, :] for r in range(TOPK)], SORT16)
    for shift in (4, 2, 1):
        v = _exchange([jnp.maximum(v[r], pltpu.roll(v[TOPK - 1 - r], shift, axis=0)) for r in range(TOPK)],
                      BITONIC16)
    return v


def _topk_kernel(xt_ref, wq_ref, keys_ref, r2_ref, e2_ref, n1_ref, c1_ref,
                 qt_ref, s_ref, a_ref, b_ref, cnt_ref, iz_ref, r1_ref, *, tm):
    NH, K = PEER_HEADS, PEER_KEYS
    qt_ref[...] = jnp.dot(wq_ref[...], xt_ref[...], preferred_element_type=F32).astype(BF16)
    for hp in range(2 * NH):
        s_ref[hp] = jnp.dot(keys_ref[hp], qt_ref[hp * PEER_HALF:(hp + 1) * PEER_HALF, :],
                            preferred_element_type=F32)

    pairs = [(i, j) for i in range(TOPK) for j in range(TOPK) if (i + 1) * (j + 1) <= TOPK]
    key_id = lax.broadcasted_iota(jnp.int32, (K, LANES), 0).astype(F32)

    def select_stats(lanes, tie_break):
        bad = jnp.zeros((1, LANES), F32)
        for h in range(NH):
            for p in range(2):
                cur = s_ref[2 * h + p, :, lanes]
                top_ref = a_ref if p == 0 else b_ref
                rank = jnp.full(cur.shape, NO_RANK, F32)
                if tie_break:
                    for j in range(TOPK):
                        m = jnp.max(cur, axis=0, keepdims=True)
                        first = jnp.min(jnp.where(cur == m, key_id, float(K)), axis=0, keepdims=True)
                        hit = key_id == first
                        top_ref[j, h:h + 1, :] = m
                        rank = jnp.where(hit, float(j), rank)
                        cur = jnp.where(hit, -jnp.inf, cur)
                    if p == 0:
                        r1_ref[h] = rank
                else:
                    top = _top16_sorted(cur)
                    for j in range(TOPK):
                        top_ref[j, h:h + 1, :] = top[j][0:1, :]
                    dup = functools.reduce(jnp.logical_or, [top[j] == top[j + 1] for j in range(TOPK - 1)])
                    reach = functools.reduce(jnp.add, [jnp.where(cur[8 * r:8 * r + 8, :] >= top[TOPK - 1], 1.0, 0.0)
                                                       for r in range(K // 8)])
                    reach = jnp.sum(reach, axis=0, keepdims=True)
                    tied = jnp.logical_or(dup[0:1, :], reach != float(TOPK))
                    bad = jnp.maximum(bad, jnp.where(tied, 1.0, 0.0))
                    if p == 1:
                        for j in reversed(range(TOPK)):
                            rank = jnp.where(cur >= jnp.concatenate([top[j]] * (K // 8), axis=0), float(j), rank)
                if p == 1:
                    r2_ref[h, :, lanes] = rank.astype(r2_ref.dtype)
        a = [a_ref[i] for i in range(TOPK)]
        b = [b_ref[j] for j in range(TOPK)]
        cand = [a[i] + b[j] for (i, j) in pairs]
        if tie_break:
            sel = []
            for ip, cp in enumerate(cand):
                ahead = jnp.zeros(cp.shape, F32)
                for iq, cq in enumerate(cand):
                    if iq != ip:
                        ahead = ahead + jnp.where((cq >= cp) if iq < ip else (cq > cp), 1.0, 0.0)
                sel.append(ahead < float(TOPK))
        else:
            cur = list(cand)
            for j in range(TOPK):
                tau = functools.reduce(jnp.maximum, cur)
                cur = [jnp.where(c == tau, -jnp.inf, c) for c in cur]
            sel = [c >= tau for c in cand]
        zsum = jnp.zeros(cand[0].shape, F32)
        cnt = [jnp.zeros(cand[0].shape, F32) for _ in range(TOPK)]
        for (i, j), cp, sp in zip(pairs, cand, sel):
            cnt[i] = cnt[i] + jnp.where(sp, 1.0, 0.0)
            zsum = zsum + jnp.where(sp, jnp.exp(cp - cand[0]), 0.0)
        for i in range(TOPK):
            cnt_ref[i] = cnt[i]
        iz_ref[...] = 0.5 / zsum
        if not tie_break:
            total = functools.reduce(jnp.add, cnt)
            bad8 = jnp.where(total != float(TOPK), 1.0, 0.0)
            bad = jnp.maximum(bad, jnp.max(bad8, axis=0, keepdims=True))
        for h in range(NH):
            s1 = s_ref[2 * h, :, lanes]
            n1 = jnp.zeros(s1.shape, F32)
            for i in range(TOPK):
                hit = (r1_ref[h] == float(i)) if tie_break else (s1 == a_ref[i, h:h + 1, :])
                n1 = jnp.where(hit, cnt_ref[i, h:h + 1, :], n1)
            n1_ref[h, :, lanes] = n1
            c1_ref[h, :, lanes] = jnp.exp(s1 - a_ref[0, h:h + 1, :]) * iz_ref[h:h + 1, :]
            s2 = s_ref[2 * h + 1, :, lanes]
            e2_ref[h, :, lanes] = jnp.exp(s2 - b_ref[0, h:h + 1, :]).astype(e2_ref.dtype)
        return bad

    def group(gi, carry):
        lanes = pl.ds(pl.multiple_of(gi * LANES, LANES), LANES)
        bad = select_stats(lanes, tie_break=False)

        @pl.when(jnp.max(bad) > 0.0)
        def _():
            select_stats(lanes, tie_break=True)

        return carry

    lax.fori_loop(0, tm // LANES, group, 0)


def _topk(xt, wq_t, keys, tm):
    D, T = xt.shape
    NH, K = PEER_HEADS, PEER_KEYS
    stat = pl.BlockSpec((NH, K, tm), lambda i: (0, 0, i))
    return pl.pallas_call(
        functools.partial(_topk_kernel, tm=tm),
        grid=(T // tm,),
        in_specs=[pl.BlockSpec((D, tm), lambda i: (0, i)),
                  pl.BlockSpec(wq_t.shape, lambda i: (0, 0)),
                  pl.BlockSpec(keys.shape, lambda i: (0, 0, 0))],
        out_specs=[stat] * 4,
        out_shape=[jax.ShapeDtypeStruct((NH, K, T), BF16), jax.ShapeDtypeStruct((NH, K, T), BF16),
                   jax.ShapeDtypeStruct((NH, K, T), F32), jax.ShapeDtypeStruct((NH, K, T), F32)],
        scratch_shapes=[pltpu.VMEM((wq_t.shape[0], tm), BF16),
                        pltpu.VMEM((2 * NH, K, tm), F32),
                        pltpu.VMEM((TOPK, NH, LANES), F32),
                        pltpu.VMEM((TOPK, NH, LANES), F32),
                        pltpu.VMEM((TOPK, NH, LANES), F32),
                        pltpu.VMEM((NH, LANES), F32),
                        pltpu.VMEM((NH, K, LANES), F32)],
        compiler_params=_params(("parallel",)),
        name="peer_topk",
    )(xt, wq_t, keys)


def _gelu2(x):
    return x + x * jnp.tanh(x * (GELU_C0 + GELU_C1 * (x * x)))


def _peer_kernel(xt_ref, u_ref, vt_ref, r2_ref, e2_ref, n1_ref, c1_ref, x1_ref, fg_ref, o_ref,
                 acc_ref, pt_ref, at_ref, *, te, tm, nsplit):
    NH, K = PEER_HEADS, PEER_KEYS
    e = pl.program_id(1)

    @pl.when(e == 0)
    def _():
        acc_ref[...] = jnp.zeros_like(acc_ref)

    n_i1 = te // K
    i1_rows = pl.ds(pl.multiple_of(e * n_i1, n_i1), n_i1)
    sub = te // nsplit

    def scores(s):
        at_ref[s] = jnp.dot(u_ref[s * sub:(s + 1) * sub, :], xt_ref[...], preferred_element_type=F32)

    def gated(s):
        nk = sub // K
        for lg in range(tm // LANES):
            lanes = slice(lg * LANES, (lg + 1) * LANES)
            gates = [None] * nk
            for h in range(NH):
                r2 = r2_ref[h, :, lanes]
                e2 = e2_ref[h, :, lanes]
                n1t = n1_ref[h, i1_rows, lanes]
                c1t = c1_ref[h, i1_rows, lanes]
                for k in range(nk):
                    kk = s * nk + k
                    n1 = jnp.broadcast_to(n1t[kk:kk + 1, :], (K, LANES)).astype(BF16)
                    c1 = jnp.broadcast_to(c1t[kk:kk + 1, :], (K, LANES)).astype(BF16)
                    term = jnp.minimum(jnp.maximum(n1 - r2, 0), e2) * c1
                    gates[k] = term if gates[k] is None else gates[k] + term
            for k in range(nk):
                kk = s * nk + k
                act = _gelu2(at_ref[s, k * K:(k + 1) * K, lanes]).astype(BF16)
                pt_ref[kk * K:(kk + 1) * K, lanes] = act * gates[k]

    def mixed(s):
        rows = slice(s * sub, (s + 1) * sub)
        for dm in range(0, acc_ref.shape[0], OUT_CHUNK):
            out = slice(dm, dm + OUT_CHUNK)
            acc_ref[out, :] += jnp.dot(vt_ref[out, rows], pt_ref[rows, :], preferred_element_type=F32)

    for s in range(nsplit):
        scores(s)
    for s in range(nsplit):
        gated(s)
        mixed(s)

    @pl.when(e == pl.num_programs(1) - 1)
    def _():
        o_ref[...] = _rms(x1_ref[...] + acc_ref[...].T, fg_ref[...])


def _peer(xt, u, vt, r2, e2, n1, c1, x1, fg, tm, te, nsplit=4):
    D, T = xt.shape
    E = u.shape[0]
    NH, K = PEER_HEADS, PEER_KEYS
    assert te == 8 * K, "a step must cover exactly one sublane tile of first-half keys"
    stat = pl.BlockSpec((NH, K, tm), lambda i, e: (0, 0, i))
    return pl.pallas_call(
        functools.partial(_peer_kernel, te=te, tm=tm, nsplit=nsplit),
        grid=(T // tm, E // te),
        in_specs=[pl.BlockSpec((D, tm), lambda i, e: (0, i)),
                  pl.BlockSpec((te, D), lambda i, e: (e, 0)),
                  pl.BlockSpec((D, te), lambda i, e: (0, e)),
                  stat, stat, stat, stat,
                  pl.BlockSpec((tm, D), lambda i, e: (i, 0)),
                  pl.BlockSpec((1, D), lambda i, e: (0, 0))],
        out_specs=pl.BlockSpec((tm, D), lambda i, e: (i, 0)),
        out_shape=jax.ShapeDtypeStruct((T, D), F32),
        scratch_shapes=[pltpu.VMEM((D, tm), F32), pltpu.VMEM((te, tm), BF16),
                        pltpu.VMEM((nsplit, te // nsplit, tm), F32)],
        compiler_params=_params(("parallel", "arbitrary")),
        name="peer_dense",
    )(xt, u, vt, r2, e2, n1, c1, x1, fg)


def _tiles(T, S):
    def pick(n, pref):
        t = min(pref, n)
        while n % t:
            t //= 2
        return t
    return dict(tm_in=pick(T, 1024), tn_in=1536, nck=pick(S // CHUNK, 16), tg=pick(T, 512),
                tm_out=pick(T, 512), tm_topk=pick(T, 512), tm_peer=pick(T, 512), te=8 * PEER_KEYS)


def _layer(x2d, B, S, norm1_g, w_in, conv_w, conv_b, b_igate, b_fgate, mlstm_norm_g,
           gmlp_ln_g, gmlp_ln_b, gmlp_w_s, gmlp_b_s, w_out, norm2_g,
           peer_w_query, peer_sub_keys, peer_u, peer_v, out_g):
    T, D = x2d.shape
    H, L, W = HEADS, CHUNK, MLSTM_W
    nc = S // L
    t = _tiles(T, S)

    g0 = 4 * W
    u0 = g0 + 2 * H
    w_main = jnp.concatenate([w_in[:, :g0], w_in[:, u0:]], axis=1).astype(BF16)
    w_gate = jnp.pad(w_in[:, g0:u0], ((0, 0), (0, LANES - 2 * H))).astype(BF16)
    z, zg = _inproj(x2d, norm1_g[None, :], w_main, w_gate, t["tm_in"], t["tn_in"])

    gcols = zg[:, :2 * H].reshape(B, nc, L, 2, H).transpose(3, 2, 0, 4, 1).reshape(2, L, B * H * nc)
    bi_row = jnp.broadcast_to(b_igate[None, :, None], (B, H, nc)).reshape(1, -1)
    bf_row = jnp.broadcast_to(b_fgate[None, :, None], (B, H, nc)).reshape(1, -1)
    cola, sc, emt, wg, r, decay = _gates(gcols[0], gcols[1], bi_row, bf_row, nc)
    cs = jnp.stack([cola, sc, emt, wg]).reshape(4, L, B, H, nc).transpose(2, 4, 1, 0, 3).reshape(B, nc, L, 4 * H)
    r_rows = jnp.pad(r.reshape(L, B, H, nc).transpose(1, 3, 2, 0), ((0, 0), (0, 0), (0, 0), (0, 2 * HEAD_DIM - L)))
    d_rows = jnp.broadcast_to(decay[0].reshape(B, H, nc).transpose(0, 2, 1)[..., None], (B, nc, H, 2 * HEAD_DIM))
    rs = jnp.concatenate([r_rows, d_rows], axis=2)
    ya = _mlstm(z, cs, rs, conv_w, conv_b[None, :], mlstm_norm_g[None, :], B, S, t["nck"])

    bs_b = jnp.broadcast_to(gmlp_b_s[:, :, None], (GROUPS, GMLP_CHUNK, GROUP_DIM))
    yb = _gmlp(z, gmlp_ln_g[None, :], gmlp_ln_b[None, :], gmlp_w_s, bs_b, t["tg"])

    w_o = w_out.astype(BF16)
    x1, xt = _outproj(x2d, ya, yb, w_o[:W], w_o[W:], norm2_g[None, :], t["tm_out"])

    wq_t = peer_w_query.T.astype(BF16)
    keys = peer_sub_keys.reshape(2 * PEER_HEADS, PEER_KEYS, PEER_HALF).astype(BF16)
    r2, e2, n1, c1 = _topk(xt, wq_t, keys, t["tm_topk"])
    return _peer(xt, peer_u.astype(BF16), peer_v.T.astype(BF16), r2, e2, n1, c1, x1,
                 out_g[None, :], t["tm_peer"], t["te"])


def kernel(x, norm1_g, w_in, conv_w, conv_b, b_igate, b_fgate, mlstm_norm_g, gmlp_ln_g, gmlp_ln_b,
           gmlp_w_s, gmlp_b_s, w_out, norm2_g, peer_w_query, peer_sub_keys, peer_u, peer_v, final_g):
    B, S, D = x.shape
    depth = norm1_g.shape[0]
    assert depth == 1, "final rmsnorm is fused into the last layer's PEER kernel"
    x2d = x.reshape(B * S, D)
    out = _layer(x2d, B, S, norm1_g[0], w_in[0], conv_w[0], conv_b[0], b_igate[0], b_fgate[0],
                 mlstm_norm_g[0], gmlp_ln_g[0], gmlp_ln_b[0], gmlp_w_s[0], gmlp_b_s[0], w_out[0],
                 norm2_g[0], peer_w_query[0], peer_sub_keys[0], peer_u[0], peer_v[0], final_g)
    return out.reshape(B, S, D)
```

```python
import functools
import math

import jax
import jax.numpy as jnp
from jax import lax
from jax.experimental import pallas as pl
from jax.experimental.pallas import tpu as pltpu

F32 = jnp.float32
BF16 = jnp.bfloat16
EPS = 1e-6

CHUNK = 64
HEADS = 4
HEAD_DIM = 128
MLSTM_W = HEADS * HEAD_DIM
CONV_K = 4
GROUPS = 4
GROUP_DIM = 128
GMLP_W = GROUPS * GROUP_DIM
GMLP_CHUNK = 128
PEER_HEADS = 8
PEER_KEYS = 128
PEER_HALF = 128
TOPK = 16
NO_RANK = 127.0
LANES = 128
BF16_ROWS = 16
OUT_CHUNK = 256
GELU_C0 = math.sqrt(2.0 / math.pi)
GELU_C1 = GELU_C0 * 0.044715

VMEM_LIMIT = 56 * 1024 * 1024


def _params(sem):
    return pltpu.CompilerParams(dimension_semantics=sem, vmem_limit_bytes=VMEM_LIMIT)


def _rms(x, g):
    return x * lax.rsqrt(jnp.mean(x * x, axis=-1, keepdims=True) + EPS) * g


def _inproj_kernel(x_ref, g_ref, w_ref, wg_ref, z_ref, gate_ref, xn_ref):
    j = pl.program_id(1)

    @pl.when(j == 0)
    def _():
        xn_ref[...] = _rms(x_ref[...], g_ref[...]).astype(BF16)

    z_ref[...] = jnp.dot(xn_ref[...], w_ref[...], preferred_element_type=F32).astype(z_ref.dtype)

    @pl.when(j == pl.num_programs(1) - 1)
    def _():
        gate_ref[...] = jnp.dot(xn_ref[...], wg_ref[...], preferred_element_type=F32)


def _inproj(x, g, w, wg, tm, tn):
    T, D = x.shape
    N = w.shape[1]
    return pl.pallas_call(
        _inproj_kernel,
        grid=(T // tm, N // tn),
        in_specs=[pl.BlockSpec((tm, D), lambda i, j: (i, 0)),
                  pl.BlockSpec((1, D), lambda i, j: (0, 0)),
                  pl.BlockSpec((D, tn), lambda i, j: (0, j)),
                  pl.BlockSpec(wg.shape, lambda i, j: (0, 0))],
        out_specs=[pl.BlockSpec((tm, tn), lambda i, j: (i, j)),
                   pl.BlockSpec((tm, wg.shape[1]), lambda i, j: (i, 0))],
        out_shape=[jax.ShapeDtypeStruct((T, N), BF16), jax.ShapeDtypeStruct((T, wg.shape[1]), F32)],
        scratch_shapes=[pltpu.VMEM((tm, D), BF16)],
        compiler_params=_params(("parallel", "arbitrary")),
        name="inproj",
    )(x, g, w, wg)


def _gate_kernel(i_ref, f_ref, bi_ref, bf_ref,
                 cola_ref, sc_ref, emt_ref, wg_ref, r_ref, decay_ref, *, nchunks):
    ig = i_ref[...] + bi_ref[...]
    lf = jax.nn.log_sigmoid(f_ref[...] + bf_ref[...])
    L, NC = ig.shape
    row = lax.broadcasted_iota(jnp.int32, (L, NC), 0)

    b = lf
    sh = 1
    while sh < L:
        b = b + jnp.where(row >= sh, pltpu.roll(b, sh, axis=0), 0.0)
        sh *= 2
    r = ig - b
    cm = r
    sh = 1
    while sh < L:
        cm = jnp.maximum(cm, jnp.where(row >= sh, pltpu.roll(cm, sh, axis=0), -jnp.inf))
        sh *= 2

    bl = jnp.broadcast_to(b[L - 1:L, :], (8, NC))
    gmax = bl + jnp.broadcast_to(cm[L - 1:L, :], (8, NC))
    lane = lax.broadcasted_iota(jnp.int32, (8, NC), 1)
    first = (lane % nchunks) == 0

    def step(_, m_new):
        m_prev = jnp.where(first, 0.0, pltpu.roll(m_new, 1, axis=1))
        return jnp.maximum(bl + m_prev, gmax)

    m_new = lax.fori_loop(0, nchunks, step, gmax)
    m_prev = jnp.where(first, 0.0, pltpu.roll(m_new, 1, axis=1))

    inter = b + m_prev[0:1, :]
    m_t = jnp.maximum(inter, b + cm)
    cola_ref[...] = b - m_t
    sc_ref[...] = jnp.exp(inter - m_t)
    emt_ref[...] = jnp.exp(-m_t)
    wg_ref[...] = jnp.exp(bl[0:1, :] + r - m_new[0:1, :])
    r_ref[...] = r
    decay_ref[...] = jnp.exp(bl + m_prev - m_new)


def _gates(icol, fcol, bi_row, bf_row, nchunks):
    L, NC = icol.shape
    full = pl.BlockSpec((L, NC), lambda: (0, 0))
    row = pl.BlockSpec((1, NC), lambda: (0, 0))
    row8 = pl.BlockSpec((8, NC), lambda: (0, 0))
    return pl.pallas_call(
        functools.partial(_gate_kernel, nchunks=nchunks),
        in_specs=[full, full, row, row],
        out_specs=[full] * 5 + [row8],
        out_shape=[jax.ShapeDtypeStruct((L, NC), F32)] * 5 + [jax.ShapeDtypeStruct((8, NC), F32)],
        name="mlstm_gates",
    )(icol, fcol, bi_row, bf_row)


def _mlstm_kernel(q_ref, k_ref, v_ref, o_ref, cs_ref, rs_ref, cw_ref, cb_ref, ng_ref, y_ref,
                  caug_ref, hq_ref, hk_ref, *, nck):
    L, H, Dh, W = CHUNK, HEADS, HEAD_DIM, MLSTM_W
    sblk = nck * L

    @pl.when(pl.program_id(1) == 0)
    def _():
        caug_ref[...] = jnp.zeros_like(caug_ref)
        hq_ref[...] = jnp.zeros_like(hq_ref)
        hk_ref[...] = jnp.zeros_like(hk_ref)

    causal = (lax.broadcasted_iota(jnp.int32, (L, L), 0) >= lax.broadcasted_iota(jnp.int32, (L, L), 1))
    lane0 = lax.broadcasted_iota(jnp.int32, (L, Dh), 1) == 0

    def conv_silu(x_ref, hist_ref, c, cw, cb):
        xc = x_ref[pl.ds(pl.multiple_of(c * L, L), L), :].astype(F32)
        start = pl.multiple_of(jnp.maximum(c * L - BF16_ROWS, 0), BF16_ROWS)
        prev = x_ref[pl.ds(start, BF16_ROWS), :].astype(F32)[BF16_ROWS - 8:, :]
        prev = jnp.where(c == 0, hist_ref[...], prev)
        ext = jnp.concatenate([prev, xc], axis=0)
        out = cb
        for j in range(CONV_K):
            sh = CONV_K - 1 - j
            xs = ext if sh == 0 else pltpu.roll(ext, sh, axis=0)
            out = out + xs[8:, :] * cw[j:j + 1, :]
        return out * jax.nn.sigmoid(out)

    def body(c, carry):
        rows = pl.ds(pl.multiple_of(c * L, L), L)
        q = conv_silu(q_ref, hq_ref, c, cw_ref[:, :W], cb_ref[:, :W])
        k = conv_silu(k_ref, hk_ref, c, cw_ref[:, W:], cb_ref[:, W:]) * (Dh ** -0.5)
        v = v_ref[rows, :].astype(F32)
        kt = k.T
        cs = cs_ref[c]
        rs = rs_ref[c]
        outs = []
        for h in range(H):
            sl = slice(h * Dh, (h + 1) * Dh)
            qh = q[:, sl].astype(BF16)
            kh = k[:, sl].astype(BF16)
            kth = kt[sl, :].astype(BF16)
            vh = v[:, sl]
            cola = cs[:, h:h + 1]
            sc = cs[:, H + h:H + h + 1]
            emt = cs[:, 2 * H + h:2 * H + h + 1]
            wg = cs[:, 3 * H + h:3 * H + h + 1]
            rowr = rs[h:h + 1, :L]
            decay = rs[H + h:H + h + 1, :]
            sqk = lax.dot_general(qh, kh, (((1,), (1,)), ((), ())), preferred_element_type=F32)
            w = jnp.exp(jnp.where(causal, cola + rowr, -jnp.inf))
            smat = sqk * w
            den_i = jnp.sum(smat, axis=-1, keepdims=True)
            caug = caug_ref[h]
            qc = jnp.dot(qh, caug.astype(BF16), preferred_element_type=F32)
            num = jnp.dot(smat.astype(BF16), vh.astype(BF16), preferred_element_type=F32) + sc * qc[:, :Dh]
            den = den_i + sc * qc[:, Dh:Dh + 1]
            hh = num / jnp.maximum(jnp.abs(den), emt)
            outs.append(hh * lax.rsqrt(jnp.mean(hh * hh, axis=-1, keepdims=True) + EPS))
            vaug = jnp.concatenate([wg * vh, jnp.where(lane0, wg, 0.0)], axis=1).astype(BF16)
            dc = jnp.dot(kth, vaug, preferred_element_type=F32)
            caug_ref[h] = decay * caug + dc
        hcat = jnp.concatenate(outs, axis=1)
        y = hcat * ng_ref[...] * jax.nn.sigmoid(o_ref[rows, :].astype(F32))
        y_ref[rows, :] = y.astype(y_ref.dtype)
        return carry

    lax.fori_loop(0, nck, body, 0)
    hq_ref[...] = q_ref[pl.ds(sblk - BF16_ROWS, BF16_ROWS), :].astype(F32)[BF16_ROWS - 8:, :]
    hk_ref[...] = k_ref[pl.ds(sblk - BF16_ROWS, BF16_ROWS), :].astype(F32)[BF16_ROWS - 8:, :]


def _mlstm(z, cs, rs, conv_w, conv_b, norm_g, B, S, nck):
    T = z.shape[0]
    L, H, Dh, W = CHUNK, HEADS, HEAD_DIM, MLSTM_W
    sblk = nck * L
    nsb = S // sblk

    def zspec(col):
        return pl.BlockSpec((sblk, W), lambda b, s, col=col: (b * nsb + s, col))

    return pl.pallas_call(
        functools.partial(_mlstm_kernel, nck=nck),
        grid=(B, nsb),
        in_specs=[zspec(0), zspec(1), zspec(2), zspec(3),
                  pl.BlockSpec((None, nck, L, 4 * H), lambda b, s: (b, s, 0, 0)),
                  pl.BlockSpec((None, nck, 2 * H, 2 * Dh), lambda b, s: (b, s, 0, 0)),
                  pl.BlockSpec((CONV_K, 2 * W), lambda b, s: (0, 0)),
                  pl.BlockSpec((1, 2 * W), lambda b, s: (0, 0)),
                  pl.BlockSpec((1, W), lambda b, s: (0, 0))],
        out_specs=pl.BlockSpec((sblk, W), lambda b, s: (b * nsb + s, 0)),
        out_shape=jax.ShapeDtypeStruct((T, W), BF16),
        scratch_shapes=[pltpu.VMEM((H, Dh, 2 * Dh), F32),
                        pltpu.VMEM((8, W), F32),
                        pltpu.VMEM((8, W), F32)],
        compiler_params=_params(("parallel", "arbitrary")),
        name="mlstm",
    )(z, z, z, z, cs, rs, conv_w, conv_b, norm_g)


def _gmlp_kernel(u_ref, v_ref, lg_ref, lb_ref, ws_ref, bs_ref, y_ref, *, nchunk):
    C, G, Dg = GMLP_CHUNK, GROUPS, GROUP_DIM
    pos_t = lax.broadcasted_iota(jnp.int32, (C, C), 0) // CHUNK
    pos_s = lax.broadcasted_iota(jnp.int32, (C, C), 1) // CHUNK
    mask = (pos_t >= pos_s).astype(F32)
    for g in range(G):
        wm = (ws_ref[g] * mask).astype(BF16)
        cols = slice(g * Dg, (g + 1) * Dg)
        for c in range(nchunk):
            rows = slice(c * C, (c + 1) * C)
            vv = jax.nn.gelu(v_ref[rows, cols].astype(F32))
            mu = jnp.mean(vv, axis=-1, keepdims=True)
            var = jnp.mean(jnp.square(vv - mu), axis=-1, keepdims=True)
            vv = (vv - mu) * lax.rsqrt(var + EPS)
            vv = vv * lg_ref[:, cols] + lb_ref[:, cols]
            mix = jnp.dot(wm, vv.astype(BF16), preferred_element_type=F32) + bs_ref[g]
            y_ref[rows, cols] = (jax.nn.gelu(u_ref[rows, cols].astype(F32)) * mix).astype(y_ref.dtype)


def _gmlp(z, ln_g, ln_b, w_s, bs_b, tg):
    T = z.shape[0]
    W, C, G = GMLP_W, GMLP_CHUNK, GROUPS
    return pl.pallas_call(
        functools.partial(_gmlp_kernel, nchunk=tg // C),
        grid=(T // tg,),
        in_specs=[pl.BlockSpec((tg, W), lambda i: (i, 4)),
                  pl.BlockSpec((tg, W), lambda i: (i, 5)),
                  pl.BlockSpec((1, W), lambda i: (0, 0)),
                  pl.BlockSpec((1, W), lambda i: (0, 0)),
                  pl.BlockSpec((G, C, C), lambda i: (0, 0, 0)),
                  pl.BlockSpec((G, C, GROUP_DIM), lambda i: (0, 0, 0))],
        out_specs=pl.BlockSpec((tg, W), lambda i: (i, 0)),
        out_shape=jax.ShapeDtypeStruct((T, W), BF16),
        compiler_params=_params(("parallel",)),
        name="gmlp",
    )(z, z, ln_g, ln_b, w_s, bs_b)


def _outproj_kernel(x_ref, ya_ref, yb_ref, wa_ref, wb_ref, g_ref, x1_ref, xt_ref):
    x1 = (x_ref[...]
          + jnp.dot(ya_ref[...], wa_ref[...], preferred_element_type=F32)
          + jnp.dot(yb_ref[...], wb_ref[...], preferred_element_type=F32))
    x1_ref[...] = x1
    xt_ref[...] = _rms(x1, g_ref[...]).T.astype(BF16)


def _outproj(x, ya, yb, wa, wb, g, tm):
    T, D = x.shape
    Wh = ya.shape[1]
    return pl.pallas_call(
        _outproj_kernel,
        grid=(T // tm,),
        in_specs=[pl.BlockSpec((tm, D), lambda i: (i, 0)),
                  pl.BlockSpec((tm, Wh), lambda i: (i, 0)),
                  pl.BlockSpec((tm, Wh), lambda i: (i, 0)),
                  pl.BlockSpec((Wh, D), lambda i: (0, 0)),
                  pl.BlockSpec((Wh, D), lambda i: (0, 0)),
                  pl.BlockSpec((1, D), lambda i: (0, 0))],
        out_specs=[pl.BlockSpec((tm, D), lambda i: (i, 0)),
                   pl.BlockSpec((D, tm), lambda i: (0, i))],
        out_shape=[jax.ShapeDtypeStruct((T, D), F32), jax.ShapeDtypeStruct((D, T), BF16)],
        compiler_params=_params(("parallel",)),
        name="outproj",
    )(x, ya, yb, wa, wb, g)


def _oddeven_mergesort(lo, hi):
    def merge(lo, hi, r):
        step = 2 * r
        if step < hi - lo:
            yield from merge(lo, hi, step)
            yield from merge(lo + r, hi, step)
            yield from ((i, i + r) for i in range(lo + r, hi - r, step))
        else:
            yield (lo, lo + r)
    if hi > lo:
        mid = lo + (hi - lo) // 2
        yield from _oddeven_mergesort(lo, mid)
        yield from _oddeven_mergesort(mid + 1, hi)
        yield from merge(lo, hi, 1)


SORT16 = tuple(_oddeven_mergesort(0, TOPK - 1))
BITONIC16 = tuple((i, i + d) for d in (8, 4, 2, 1) for i in range(TOPK) if not i & d)


def _exchange(v, net):
    v = list(v)
    for i, j in net:
        v[i], v[j] = jnp.maximum(v[i], v[j]), jnp.minimum(v[i], v[j])
    return v


def _top16_sorted(s):
    v = _exchange([s[8 * r:8 * r + 8, :] for r in range(TOPK)], SORT16)
    for shift in (4, 2, 1):
        v = _exchange([jnp.maximum(v[r], pltpu.roll(v[TOPK - 1 - r], shift, axis=0)) for r in range(TOPK)],
                      BITONIC16)
    return v


def _topk_kernel(xt_ref, wq_ref, keys_ref, r2_ref, e2_ref, n1_ref, c1_ref,
                 qt_ref, s_ref, a_ref, b_ref, cnt_ref, iz_ref, r1_ref, *, tm):
    NH, K = PEER_HEADS, PEER_KEYS
    qt_ref[...] = jnp.dot(wq_ref[...], xt_ref[...], preferred_element_type=F32).astype(BF16)
    for hp in range(2 * NH):
        s_ref[hp] = jnp.dot(keys_ref[hp], qt_ref[hp * PEER_HALF:(hp + 1) * PEER_HALF, :],
                            preferred_element_type=F32)

    pairs = [(i, j) for i in range(TOPK) for j in range(TOPK) if (i + 1) * (j + 1) <= TOPK]
    key_id = lax.broadcasted_iota(jnp.int32, (K, LANES), 0).astype(F32)

    def select_stats(lanes, tie_break):
        bad = jnp.zeros((1, LANES), F32)
        for h in range(NH):
            for p in range(2):
                cur = s_ref[2 * h + p, :, lanes]
                top_ref = a_ref if p == 0 else b_ref
                rank = jnp.full(cur.shape, NO_RANK, F32)
                if tie_break:
                    for j in range(TOPK):
                        m = jnp.max(cur, axis=0, keepdims=True)
                        first = jnp.min(jnp.where(cur == m, key_id, float(K)), axis=0, keepdims=True)
                        hit = key_id == first
                        top_ref[j, h:h + 1, :] = m
                        rank = jnp.where(hit, float(j), rank)
                        cur = jnp.where(hit, -jnp.inf, cur)
                    if p == 0:
                        r1_ref[h] = rank
                else:
                    top = _top16_sorted(cur)
                    for j in range(TOPK):
                        top_ref[j, h:h + 1, :] = top[j][0:1, :]
                    dup = functools.reduce(jnp.logical_or, [top[j] == top[j + 1] for j in range(TOPK - 1)])
                    reach = functools.reduce(jnp.add, [jnp.where(cur[8 * r:8 * r + 8, :] >= top[TOPK - 1], 1.0, 0.0)
                                                       for r in range(K // 8)])
                    reach = jnp.sum(reach, axis=0, keepdims=True)
                    tied = jnp.logical_or(dup[0:1, :], reach != float(TOPK))
                    bad = jnp.maximum(bad, jnp.where(tied, 1.0, 0.0))
                    if p == 1:
                        for j in reversed(range(TOPK)):
                            rank = jnp.where(cur >= jnp.concatenate([top[j]] * (K // 8), axis=0), float(j), rank)
                if p == 1:
                    r2_ref[h, :, lanes] = rank.astype(r2_ref.dtype)
        a = [a_ref[i] for i in range(TOPK)]
        b = [b_ref[j] for j in range(TOPK)]
        cand = [a[i] + b[j] for (i, j) in pairs]
        if tie_break:
            sel = []
            for ip, cp in enumerate(cand):
                ahead = jnp.zeros(cp.shape, F32)
                for iq, cq in enumerate(cand):
                    if iq != ip:
                        ahead = ahead + jnp.where((cq >= cp) if iq < ip else (cq > cp), 1.0, 0.0)
                sel.append(ahead < float(TOPK))
        else:
            cur = list(cand)
            for j in range(TOPK):
                tau = functools.reduce(jnp.maximum, cur)
                cur = [jnp.where(c == tau, -jnp.inf, c) for c in cur]
            sel = [c >= tau for c in cand]
        zsum = jnp.zeros(cand[0].shape, F32)
        cnt = [jnp.zeros(cand[0].shape, F32) for _ in range(TOPK)]
        for (i, j), cp, sp in zip(pairs, cand, sel):
            cnt[i] = cnt[i] + jnp.where(sp, 1.0, 0.0)
            zsum = zsum + jnp.where(sp, jnp.exp(cp - cand[0]), 0.0)
        for i in range(TOPK):
            cnt_ref[i] = cnt[i]
        iz_ref[...] = 0.5 / zsum
        if not tie_break:
            total = functools.reduce(jnp.add, cnt)
            bad8 = jnp.where(total != float(TOPK), 1.0, 0.0)
            bad = jnp.maximum(bad, jnp.max(bad8, axis=0, keepdims=True))
        for h in range(NH):
            s1 = s_ref[2 * h, :, lanes]
            n1 = jnp.zeros(s1.shape, F32)
            for i in range(TOPK):
                hit = (r1_ref[h] == float(i)) if tie_break else (s1 == a_ref[i, h:h + 1, :])
                n1 = jnp.where(hit, cnt_ref[i, h:h + 1, :], n1)
            n1_ref[h, :, lanes] = n1
            c1_ref[h, :, lanes] = jnp.exp(s1 - a_ref[0, h:h + 1, :]) * iz_ref[h:h + 1, :]
            s2 = s_ref[2 * h + 1, :, lanes]
            e2_ref[h, :, lanes] = jnp.exp(s2 - b_ref[0, h:h + 1, :]).astype(e2_ref.dtype)
        return bad

    def group(gi, carry):
        lanes = pl.ds(pl.multiple_of(gi * LANES, LANES), LANES)
        bad = select_stats(lanes, tie_break=False)

        @pl.when(jnp.max(bad) > 0.0)
        def _():
            select_stats(lanes, tie_break=True)

        return carry

    lax.fori_loop(0, tm // LANES, group, 0)


def _topk(xt, wq_t, keys, tm):
    D, T = xt.shape
    NH, K = PEER_HEADS, PEER_KEYS
    stat = pl.BlockSpec((NH, K, tm), lambda i: (0, 0, i))
    return pl.pallas_call(
        functools.partial(_topk_kernel, tm=tm),
        grid=(T // tm,),
        in_specs=[pl.BlockSpec((D, tm), lambda i: (0, i)),
                  pl.BlockSpec(wq_t.shape, lambda i: (0, 0)),
                  pl.BlockSpec(keys.shape, lambda i: (0, 0, 0))],
        out_specs=[stat] * 4,
        out_shape=[jax.ShapeDtypeStruct((NH, K, T), BF16), jax.ShapeDtypeStruct((NH, K, T), BF16),
                   jax.ShapeDtypeStruct((NH, K, T), F32), jax.ShapeDtypeStruct((NH, K, T), F32)],
        scratch_shapes=[pltpu.VMEM((wq_t.shape[0], tm), BF16),
                        pltpu.VMEM((2 * NH, K, tm), F32),
                        pltpu.VMEM((TOPK, NH, LANES), F32),
                        pltpu.VMEM((TOPK, NH, LANES), F32),
                        pltpu.VMEM((TOPK, NH, LANES), F32),
                        pltpu.VMEM((NH, LANES), F32),
                        pltpu.VMEM((NH, K, LANES), F32)],
        compiler_params=_params(("parallel",)),
        name="peer_topk",
    )(xt, wq_t, keys)


def _gelu2(x):
    return x + x * jnp.tanh(x * (GELU_C0 + GELU_C1 * (x * x)))


def _peer_kernel(xt_ref, u_ref, vt_ref, r2_ref, e2_ref, n1_ref, c1_ref, x1_ref, fg_ref, o_ref,
                 acc_ref, pt_ref, at_ref, *, te, tm, nsplit):
    NH, K = PEER_HEADS, PEER_KEYS
    e = pl.program_id(1)

    @pl.when(e == 0)
    def _():
        acc_ref[...] = jnp.zeros_like(acc_ref)

    n_i1 = te // K
    i1_rows = pl.ds(pl.multiple_of(e * n_i1, n_i1), n_i1)
    sub = te // nsplit

    def scores(s):
        at_ref[s] = jnp.dot(u_ref[s * sub:(s + 1) * sub, :], xt_ref[...], preferred_element_type=F32)

    def gated(s):
        nk = sub // K
        for lg in range(tm // LANES):
            lanes = slice(lg * LANES, (lg + 1) * LANES)
            gates = [None] * nk
            for h in range(NH):
                r2 = r2_ref[h, :, lanes]
                e2 = e2_ref[h, :, lanes]
                n1t = n1_ref[h, i1_rows, lanes]
                c1t = c1_ref[h, i1_rows, lanes]
                for k in range(nk):
                    kk = s * nk + k
                    n1 = jnp.broadcast_to(n1t[kk:kk + 1, :], (K, LANES)).astype(BF16)
                    c1 = jnp.broadcast_to(c1t[kk:kk + 1, :], (K, LANES)).astype(BF16)
                    term = jnp.minimum(jnp.maximum(n1 - r2, 0), e2) * c1
                    gates[k] = term if gates[k] is None else gates[k] + term
            for k in range(nk):
                kk = s * nk + k
                act = _gelu2(at_ref[s, k * K:(k + 1) * K, lanes]).astype(BF16)
                pt_ref[kk * K:(kk + 1) * K, lanes] = act * gates[k]

    def mixed(s):
        rows = slice(s * sub, (s + 1) * sub)
        for dm in range(0, acc_ref.shape[0], OUT_CHUNK):
            out = slice(dm, dm + OUT_CHUNK)
            acc_ref[out, :] += jnp.dot(vt_ref[out, rows], pt_ref[rows, :], preferred_element_type=F32)

    for s in range(nsplit):
        scores(s)
    for s in range(nsplit):
        gated(s)
        mixed(s)

    @pl.when(e == pl.num_programs(1) - 1)
    def _():
        o_ref[...] = _rms(x1_ref[...] + acc_ref[...].T, fg_ref[...])


def _peer(xt, u, vt, r2, e2, n1, c1, x1, fg, tm, te, nsplit=8):
    D, T = xt.shape
    E = u.shape[0]
    NH, K = PEER_HEADS, PEER_KEYS
    assert te % (8 * K) == 0, "a step must cover whole sublane tiles of first-half keys"
    stat = pl.BlockSpec((NH, K, tm), lambda i, e: (0, 0, i))
    return pl.pallas_call(
        functools.partial(_peer_kernel, te=te, tm=tm, nsplit=nsplit),
        grid=(T // tm, E // te),
        in_specs=[pl.BlockSpec((D, tm), lambda i, e: (0, i)),
                  pl.BlockSpec((te, D), lambda i, e: (e, 0)),
                  pl.BlockSpec((D, te), lambda i, e: (0, e)),
                  stat, stat, stat, stat,
                  pl.BlockSpec((tm, D), lambda i, e: (i, 0)),
                  pl.BlockSpec((1, D), lambda i, e: (0, 0))],
        out_specs=pl.BlockSpec((tm, D), lambda i, e: (i, 0)),
        out_shape=jax.ShapeDtypeStruct((T, D), F32),
        scratch_shapes=[pltpu.VMEM((D, tm), F32), pltpu.VMEM((te, tm), BF16),
                        pltpu.VMEM((nsplit, te // nsplit, tm), F32)],
        compiler_params=_params(("parallel", "arbitrary")),
        name="peer_dense",
    )(xt, u, vt, r2, e2, n1, c1, x1, fg)


def _tiles(T, S):
    def pick(n, pref):
        t = min(pref, n)
        while n % t:
            t //= 2
        return t
    return dict(tm_in=pick(T, 1024), tn_in=1536, nck=pick(S // CHUNK, 16), tg=pick(T, 512),
                tm_out=pick(T, 512), tm_topk=pick(T, 512), tm_peer=pick(T, 512), te=16 * PEER_KEYS)


def _layer(x2d, B, S, norm1_g, w_in, conv_w, conv_b, b_igate, b_fgate, mlstm_norm_g,
           gmlp_ln_g, gmlp_ln_b, gmlp_w_s, gmlp_b_s, w_out, norm2_g,
           peer_w_query, peer_sub_keys, peer_u, peer_v, out_g):
    T, D = x2d.shape
    H, L, W = HEADS, CHUNK, MLSTM_W
    nc = S // L
    t = _tiles(T, S)

    g0 = 4 * W
    u0 = g0 + 2 * H
    w_main = jnp.concatenate([w_in[:, :g0], w_in[:, u0:]], axis=1).astype(BF16)
    w_gate = jnp.pad(w_in[:, g0:u0], ((0, 0), (0, LANES - 2 * H))).astype(BF16)
    z, zg = _inproj(x2d, norm1_g[None, :], w_main, w_gate, t["tm_in"], t["tn_in"])

    gcols = zg[:, :2 * H].reshape(B, nc, L, 2, H).transpose(3, 2, 0, 4, 1).reshape(2, L, B * H * nc)
    bi_row = jnp.broadcast_to(b_igate[None, :, None], (B, H, nc)).reshape(1, -1)
    bf_row = jnp.broadcast_to(b_fgate[None, :, None], (B, H, nc)).reshape(1, -1)
    cola, sc, emt, wg, r, decay = _gates(gcols[0], gcols[1], bi_row, bf_row, nc)
    cs = jnp.stack([cola, sc, emt, wg]).reshape(4, L, B, H, nc).transpose(2, 4, 1, 0, 3).reshape(B, nc, L, 4 * H)
    r_rows = jnp.pad(r.reshape(L, B, H, nc).transpose(1, 3, 2, 0), ((0, 0), (0, 0), (0, 0), (0, 2 * HEAD_DIM - L)))
    d_rows = jnp.broadcast_to(decay[0].reshape(B, H, nc).transpose(0, 2, 1)[..., None], (B, nc, H, 2 * HEAD_DIM))
    rs = jnp.concatenate([r_rows, d_rows], axis=2)
    ya = _mlstm(z, cs, rs, conv_w, conv_b[None, :], mlstm_norm_g[None, :], B, S, t["nck"])

    bs_b = jnp.broadcast_to(gmlp_b_s[:, :, None], (GROUPS, GMLP_CHUNK, GROUP_DIM))
    yb = _gmlp(z, gmlp_ln_g[None, :], gmlp_ln_b[None, :], gmlp_w_s, bs_b, t["tg"])

    w_o = w_out.astype(BF16)
    x1, xt = _outproj(x2d, ya, yb, w_o[:W], w_o[W:], norm2_g[None, :], t["tm_out"])

    wq_t = peer_w_query.T.astype(BF16)
    keys = peer_sub_keys.reshape(2 * PEER_HEADS, PEER_KEYS, PEER_HALF).astype(BF16)
    r2, e2, n1, c1 = _topk(xt, wq_t, keys, t["tm_topk"])
    return _peer(xt, peer_u.astype(BF16), peer_v.T.astype(BF16), r2, e2, n1, c1, x1,
                 out_g[None, :], t["tm_peer"], t["te"])


def kernel(x, norm1_g, w_in, conv_w, conv_b, b_igate, b_fgate, mlstm_norm_g, gmlp_ln_g, gmlp_ln_b,
           gmlp_w_s, gmlp_b_s, w_out, norm2_g, peer_w_query, peer_sub_keys, peer_u, peer_v, final_g):
    B, S, D = x.shape
    depth = norm1_g.shape[0]
    assert depth == 1, "final rmsnorm is fused into the last layer's PEER kernel"
    x2d = x.reshape(B * S, D)
    out = _layer(x2d, B, S, norm1_g[0], w_in[0], conv_w[0], conv_b[0], b_igate[0], b_fgate[0],
                 mlstm_norm_g[0], gmlp_ln_g[0], gmlp_ln_b[0], gmlp_w_s[0], gmlp_b_s[0], w_out[0],
                 norm2_g[0], peer_w_query[0], peer_sub_keys[0], peer_u[0], peer_v[0], final_g)
    return out.reshape(B, S, D)
```

```python
import functools
import math

import jax
import jax.numpy as jnp
from jax import lax
from jax.experimental import pallas as pl
from jax.experimental.pallas import tpu as pltpu

F32 = jnp.float32
BF16 = jnp.bfloat16
EPS = 1e-6

CHUNK = 64
HEADS = 4
HEAD_DIM = 128
MLSTM_W = HEADS * HEAD_DIM
CONV_K = 4
GROUPS = 4
GROUP_DIM = 128
GMLP_W = GROUPS * GROUP_DIM
GMLP_CHUNK = 128
PEER_HEADS = 8
PEER_KEYS = 128
PEER_HALF = 128
TOPK = 16
NO_RANK = 127.0
LANES = 128
BF16_ROWS = 16
OUT_CHUNK = 256
MIX_GROUP = 2
GELU_C0 = math.sqrt(2.0 / math.pi)
GELU_C1 = GELU_C0 * 0.044715

VMEM_LIMIT = 56 * 1024 * 1024


def _params(sem):
    return pltpu.CompilerParams(dimension_semantics=sem, vmem_limit_bytes=VMEM_LIMIT)


def _rms(x, g):
    return x * lax.rsqrt(jnp.mean(x * x, axis=-1, keepdims=True) + EPS) * g


def _inproj_kernel(x_ref, g_ref, w_ref, wg_ref, z_ref, gate_ref, xn_ref):
    j = pl.program_id(1)

    @pl.when(j == 0)
    def _():
        xn_ref[...] = _rms(x_ref[...], g_ref[...]).astype(BF16)

    z_ref[...] = jnp.dot(xn_ref[...], w_ref[...], preferred_element_type=F32).astype(z_ref.dtype)

    @pl.when(j == pl.num_programs(1) - 1)
    def _():
        gate_ref[...] = jnp.dot(xn_ref[...], wg_ref[...], preferred_element_type=F32)


def _inproj(x, g, w, wg, tm, tn):
    T, D = x.shape
    N = w.shape[1]
    return pl.pallas_call(
        _inproj_kernel,
        grid=(T // tm, N // tn),
        in_specs=[pl.BlockSpec((tm, D), lambda i, j: (i, 0)),
                  pl.BlockSpec((1, D), lambda i, j: (0, 0)),
                  pl.BlockSpec((D, tn), lambda i, j: (0, j)),
                  pl.BlockSpec(wg.shape, lambda i, j: (0, 0))],
        out_specs=[pl.BlockSpec((tm, tn), lambda i, j: (i, j)),
                   pl.BlockSpec((tm, wg.shape[1]), lambda i, j: (i, 0))],
        out_shape=[jax.ShapeDtypeStruct((T, N), BF16), jax.ShapeDtypeStruct((T, wg.shape[1]), F32)],
        scratch_shapes=[pltpu.VMEM((tm, D), BF16)],
        compiler_params=_params(("parallel", "arbitrary")),
        name="inproj",
    )(x, g, w, wg)


def _gate_kernel(i_ref, f_ref, bi_ref, bf_ref,
                 cola_ref, sc_ref, emt_ref, wg_ref, r_ref, decay_ref, *, nchunks):
    ig = i_ref[...] + bi_ref[...]
    lf = jax.nn.log_sigmoid(f_ref[...] + bf_ref[...])
    L, NC = ig.shape
    row = lax.broadcasted_iota(jnp.int32, (L, NC), 0)

    b = lf
    sh = 1
    while sh < L:
        b = b + jnp.where(row >= sh, pltpu.roll(b, sh, axis=0), 0.0)
        sh *= 2
    r = ig - b
    cm = r
    sh = 1
    while sh < L:
        cm = jnp.maximum(cm, jnp.where(row >= sh, pltpu.roll(cm, sh, axis=0), -jnp.inf))
        sh *= 2

    bl = jnp.broadcast_to(b[L - 1:L, :], (8, NC))
    gmax = bl + jnp.broadcast_to(cm[L - 1:L, :], (8, NC))
    lane = lax.broadcasted_iota(jnp.int32, (8, NC), 1)
    first = (lane % nchunks) == 0

    def step(_, m_new):
        m_prev = jnp.where(first, 0.0, pltpu.roll(m_new, 1, axis=1))
        return jnp.maximum(bl + m_prev, gmax)

    m_new = lax.fori_loop(0, nchunks, step, gmax)
    m_prev = jnp.where(first, 0.0, pltpu.roll(m_new, 1, axis=1))

    inter = b + m_prev[0:1, :]
    m_t = jnp.maximum(inter, b + cm)
    cola_ref[...] = b - m_t
    sc_ref[...] = jnp.exp(inter - m_t)
    emt_ref[...] = jnp.exp(-m_t)
    wg_ref[...] = jnp.exp(bl[0:1, :] + r - m_new[0:1, :])
    r_ref[...] = r
    decay_ref[...] = jnp.exp(bl + m_prev - m_new)


def _gates(icol, fcol, bi_row, bf_row, nchunks):
    L, NC = icol.shape
    full = pl.BlockSpec((L, NC), lambda: (0, 0))
    row = pl.BlockSpec((1, NC), lambda: (0, 0))
    row8 = pl.BlockSpec((8, NC), lambda: (0, 0))
    return pl.pallas_call(
        functools.partial(_gate_kernel, nchunks=nchunks),
        in_specs=[full, full, row, row],
        out_specs=[full] * 5 + [row8],
        out_shape=[jax.ShapeDtypeStruct((L, NC), F32)] * 5 + [jax.ShapeDtypeStruct((8, NC), F32)],
        name="mlstm_gates",
    )(icol, fcol, bi_row, bf_row)


def _mlstm_kernel(q_ref, k_ref, v_ref, o_ref, cs_ref, rs_ref, cw_ref, cb_ref, ng_ref, y_ref,
                  caug_ref, hq_ref, hk_ref, *, nck):
    L, H, Dh, W = CHUNK, HEADS, HEAD_DIM, MLSTM_W
    sblk = nck * L

    @pl.when(pl.program_id(1) == 0)
    def _():
        caug_ref[...] = jnp.zeros_like(caug_ref)
        hq_ref[...] = jnp.zeros_like(hq_ref)
        hk_ref[...] = jnp.zeros_like(hk_ref)

    causal = (lax.broadcasted_iota(jnp.int32, (L, L), 0) >= lax.broadcasted_iota(jnp.int32, (L, L), 1))
    lane0 = lax.broadcasted_iota(jnp.int32, (L, Dh), 1) == 0

    def conv_silu(x_ref, hist_ref, c, cw, cb):
        xc = x_ref[pl.ds(pl.multiple_of(c * L, L), L), :].astype(F32)
        start = pl.multiple_of(jnp.maximum(c * L - BF16_ROWS, 0), BF16_ROWS)
        prev = x_ref[pl.ds(start, BF16_ROWS), :].astype(F32)[BF16_ROWS - 8:, :]
        prev = jnp.where(c == 0, hist_ref[...], prev)
        ext = jnp.concatenate([prev, xc], axis=0)
        out = cb
        for j in range(CONV_K):
            sh = CONV_K - 1 - j
            xs = ext if sh == 0 else pltpu.roll(ext, sh, axis=0)
            out = out + xs[8:, :] * cw[j:j + 1, :]
        return out * jax.nn.sigmoid(out)

    def body(c, carry):
        rows = pl.ds(pl.multiple_of(c * L, L), L)
        q = conv_silu(q_ref, hq_ref, c, cw_ref[:, :W], cb_ref[:, :W])
        k = conv_silu(k_ref, hk_ref, c, cw_ref[:, W:], cb_ref[:, W:]) * (Dh ** -0.5)
        v = v_ref[rows, :].astype(F32)
        kt = k.T
        cs = cs_ref[c]
        rs = rs_ref[c]
        outs = []
        for h in range(H):
            sl = slice(h * Dh, (h + 1) * Dh)
            qh = q[:, sl].astype(BF16)
            kh = k[:, sl].astype(BF16)
            kth = kt[sl, :].astype(BF16)
            vh = v[:, sl]
            cola = cs[:, h:h + 1]
            sc = cs[:, H + h:H + h + 1]
            emt = cs[:, 2 * H + h:2 * H + h + 1]
            wg = cs[:, 3 * H + h:3 * H + h + 1]
            rowr = rs[h:h + 1, :L]
            decay = rs[H + h:H + h + 1, :]
            sqk = lax.dot_general(qh, kh, (((1,), (1,)), ((), ())), preferred_element_type=F32)
            w = jnp.exp(jnp.where(causal, cola + rowr, -jnp.inf))
            smat = sqk * w
            den_i = jnp.sum(smat, axis=-1, keepdims=True)
            caug = caug_ref[h]
            qc = jnp.dot(qh, caug.astype(BF16), preferred_element_type=F32)
            num = jnp.dot(smat.astype(BF16), vh.astype(BF16), preferred_element_type=F32) + sc * qc[:, :Dh]
            den = den_i + sc * qc[:, Dh:Dh + 1]
            hh = num / jnp.maximum(jnp.abs(den), emt)
            outs.append(hh * lax.rsqrt(jnp.mean(hh * hh, axis=-1, keepdims=True) + EPS))
            vaug = jnp.concatenate([wg * vh, jnp.where(lane0, wg, 0.0)], axis=1).astype(BF16)
            dc = jnp.dot(kth, vaug, preferred_element_type=F32)
            caug_ref[h] = decay * caug + dc
        hcat = jnp.concatenate(outs, axis=1)
        y = hcat * ng_ref[...] * jax.nn.sigmoid(o_ref[rows, :].astype(F32))
        y_ref[rows, :] = y.astype(y_ref.dtype)
        return carry

    lax.fori_loop(0, nck, body, 0)
    hq_ref[...] = q_ref[pl.ds(sblk - BF16_ROWS, BF16_ROWS), :].astype(F32)[BF16_ROWS - 8:, :]
    hk_ref[...] = k_ref[pl.ds(sblk - BF16_ROWS, BF16_ROWS), :].astype(F32)[BF16_ROWS - 8:, :]


def _mlstm(z, cs, rs, conv_w, conv_b, norm_g, B, S, nck):
    T = z.shape[0]
    L, H, Dh, W = CHUNK, HEADS, HEAD_DIM, MLSTM_W
    sblk = nck * L
    nsb = S // sblk

    def zspec(col):
        return pl.BlockSpec((sblk, W), lambda b, s, col=col: (b * nsb + s, col))

    return pl.pallas_call(
        functools.partial(_mlstm_kernel, nck=nck),
        grid=(B, nsb),
        in_specs=[zspec(0), zspec(1), zspec(2), zspec(3),
                  pl.BlockSpec((None, nck, L, 4 * H), lambda b, s: (b, s, 0, 0)),
                  pl.BlockSpec((None, nck, 2 * H, 2 * Dh), lambda b, s: (b, s, 0, 0)),
                  pl.BlockSpec((CONV_K, 2 * W), lambda b, s: (0, 0)),
                  pl.BlockSpec((1, 2 * W), lambda b, s: (0, 0)),
                  pl.BlockSpec((1, W), lambda b, s: (0, 0))],
        out_specs=pl.BlockSpec((sblk, W), lambda b, s: (b * nsb + s, 0)),
        out_shape=jax.ShapeDtypeStruct((T, W), BF16),
        scratch_shapes=[pltpu.VMEM((H, Dh, 2 * Dh), F32),
                        pltpu.VMEM((8, W), F32),
                        pltpu.VMEM((8, W), F32)],
        compiler_params=_params(("parallel", "arbitrary")),
        name="mlstm",
    )(z, z, z, z, cs, rs, conv_w, conv_b, norm_g)


def _gmlp_kernel(u_ref, v_ref, lg_ref, lb_ref, ws_ref, bs_ref, y_ref, *, nchunk):
    C, G, Dg = GMLP_CHUNK, GROUPS, GROUP_DIM
    pos_t = lax.broadcasted_iota(jnp.int32, (C, C), 0) // CHUNK
    pos_s = lax.broadcasted_iota(jnp.int32, (C, C), 1) // CHUNK
    mask = (pos_t >= pos_s).astype(F32)
    for g in range(G):
        wm = (ws_ref[g] * mask).astype(BF16)
        cols = slice(g * Dg, (g + 1) * Dg)
        for c in range(nchunk):
            rows = slice(c * C, (c + 1) * C)
            vv = jax.nn.gelu(v_ref[rows, cols].astype(F32))
            mu = jnp.mean(vv, axis=-1, keepdims=True)
            var = jnp.mean(jnp.square(vv - mu), axis=-1, keepdims=True)
            vv = (vv - mu) * lax.rsqrt(var + EPS)
            vv = vv * lg_ref[:, cols] + lb_ref[:, cols]
            mix = jnp.dot(wm, vv.astype(BF16), preferred_element_type=F32) + bs_ref[g]
            y_ref[rows, cols] = (jax.nn.gelu(u_ref[rows, cols].astype(F32)) * mix).astype(y_ref.dtype)


def _gmlp(z, ln_g, ln_b, w_s, bs_b, tg):
    T = z.shape[0]
    W, C, G = GMLP_W, GMLP_CHUNK, GROUPS
    return pl.pallas_call(
        functools.partial(_gmlp_kernel, nchunk=tg // C),
        grid=(T // tg,),
        in_specs=[pl.BlockSpec((tg, W), lambda i: (i, 4)),
                  pl.BlockSpec((tg, W), lambda i: (i, 5)),
                  pl.BlockSpec((1, W), lambda i: (0, 0)),
                  pl.BlockSpec((1, W), lambda i: (0, 0)),
                  pl.BlockSpec((G, C, C), lambda i: (0, 0, 0)),
                  pl.BlockSpec((G, C, GROUP_DIM), lambda i: (0, 0, 0))],
        out_specs=pl.BlockSpec((tg, W), lambda i: (i, 0)),
        out_shape=jax.ShapeDtypeStruct((T, W), BF16),
        compiler_params=_params(("parallel",)),
        name="gmlp",
    )(z, z, ln_g, ln_b, w_s, bs_b)


def _outproj_kernel(x_ref, ya_ref, yb_ref, wa_ref, wb_ref, g_ref, x1_ref, xt_ref):
    x1 = (x_ref[...]
          + jnp.dot(ya_ref[...], wa_ref[...], preferred_element_type=F32)
          + jnp.dot(yb_ref[...], wb_ref[...], preferred_element_type=F32))
    x1_ref[...] = x1
    xt_ref[...] = _rms(x1, g_ref[...]).T.astype(BF16)


def _outproj(x, ya, yb, wa, wb, g, tm):
    T, D = x.shape
    Wh = ya.shape[1]
    return pl.pallas_call(
        _outproj_kernel,
        grid=(T // tm,),
        in_specs=[pl.BlockSpec((tm, D), lambda i: (i, 0)),
                  pl.BlockSpec((tm, Wh), lambda i: (i, 0)),
                  pl.BlockSpec((tm, Wh), lambda i: (i, 0)),
                  pl.BlockSpec((Wh, D), lambda i: (0, 0)),
                  pl.BlockSpec((Wh, D), lambda i: (0, 0)),
                  pl.BlockSpec((1, D), lambda i: (0, 0))],
        out_specs=[pl.BlockSpec((tm, D), lambda i: (i, 0)),
                   pl.BlockSpec((D, tm), lambda i: (0, i))],
        out_shape=[jax.ShapeDtypeStruct((T, D), F32), jax.ShapeDtypeStruct((D, T), BF16)],
        compiler_params=_params(("parallel",)),
        name="outproj",
    )(x, ya, yb, wa, wb, g)


def _oddeven_mergesort(lo, hi):
    def merge(lo, hi, r):
        step = 2 * r
        if step < hi - lo:
            yield from merge(lo, hi, step)
            yield from merge(lo + r, hi, step)
            yield from ((i, i + r) for i in range(lo + r, hi - r, step))
        else:
            yield (lo, lo + r)
    if hi > lo:
        mid = lo + (hi - lo) // 2
        yield from _oddeven_mergesort(lo, mid)
        yield from _oddeven_mergesort(mid + 1, hi)
        yield from merge(lo, hi, 1)


SORT16 = tuple(_oddeven_mergesort(0, TOPK - 1))
BITONIC16 = tuple((i, i + d) for d in (8, 4, 2, 1) for i in range(TOPK) if not i & d)


def _exchange(v, net):
    v = list(v)
    for i, j in net:
        if v[i] is None:
            v[i], v[j] = v[j], None
        elif v[j] is not None:
            v[i], v[j] = jnp.maximum(v[i], v[j]), jnp.minimum(v[i], v[j])
    return v


def _merge_top16(a, b):
    def larger(x, y):
        return y if x is None else x if y is None else jnp.maximum(x, y)
    return _exchange([larger(a[r], b[TOPK - 1 - r]) for r in range(TOPK)], BITONIC16)


def _kth_pair_sum(row):
    pad = lambda run: run + [None] * (TOPK - len(run))
    ones = [row[i][0] for i in range(8, TOPK)]
    g0 = row[0]
    g1 = _exchange(row[1] + ones[::-1], BITONIC16)
    g2 = _exchange(row[2] + row[3] + row[4] + row[5] + row[6], SORT16)
    g3 = pad(row[7])
    return _merge_top16(_merge_top16(g0, g1), _merge_top16(g2, g3))[TOPK - 1]


def _top16_sorted(s):
    v = _exchange([s[8 * r:8 * r + 8, :] for r in range(TOPK)], SORT16)
    for shift in (4, 2, 1):
        v = _exchange([jnp.maximum(v[r], pltpu.roll(v[TOPK - 1 - r], shift, axis=0)) for r in range(TOPK)],
                      BITONIC16)
    return v


def _topk_kernel(xt_ref, wq_ref, keys_ref, r2_ref, e2_ref, n1_ref, c1_ref,
                 qt_ref, s_ref, a_ref, b_ref, cnt_ref, iz_ref, r1_ref, *, tm):
    NH, K = PEER_HEADS, PEER_KEYS
    qt_ref[...] = jnp.dot(wq_ref[...], xt_ref[...], preferred_element_type=F32).astype(BF16)
    for hp in range(2 * NH):
        s_ref[hp] = jnp.dot(keys_ref[hp], qt_ref[hp * PEER_HALF:(hp + 1) * PEER_HALF, :],
                            preferred_element_type=F32)

    pairs = [(i, j) for i in range(TOPK) for j in range(TOPK) if (i + 1) * (j + 1) <= TOPK]
    key_id = lax.broadcasted_iota(jnp.int32, (K, LANES), 0).astype(F32)

    def select_stats(lanes, tie_break):
        bad = jnp.zeros((1, LANES), F32)
        for h in range(NH):
            for p in range(2):
                cur = s_ref[2 * h + p, :, lanes]
                top_ref = a_ref if p == 0 else b_ref
                rank = jnp.full(cur.shape, NO_RANK, F32)
                if tie_break:
                    for j in range(TOPK):
                        m = jnp.max(cur, axis=0, keepdims=True)
                        first = jnp.min(jnp.where(cur == m, key_id, float(K)), axis=0, keepdims=True)
                        hit = key_id == first
                        top_ref[j, h:h + 1, :] = m
                        rank = jnp.where(hit, float(j), rank)
                        cur = jnp.where(hit, -jnp.inf, cur)
                    if p == 0:
                        r1_ref[h] = rank
                else:
                    top = _top16_sorted(cur)
                    for j in range(TOPK):
                        top_ref[j, h:h + 1, :] = top[j][0:1, :]
                    dup = functools.reduce(jnp.logical_or, [top[j] == top[j + 1] for j in range(TOPK - 1)])
                    reach = functools.reduce(jnp.add, [jnp.where(cur[8 * r:8 * r + 8, :] >= top[TOPK - 1], 1.0, 0.0)
                                                       for r in range(K // 8)])
                    reach = jnp.sum(reach, axis=0, keepdims=True)
                    tied = jnp.logical_or(dup[0:1, :], reach != float(TOPK))
                    bad = jnp.maximum(bad, jnp.where(tied, 1.0, 0.0))
                    if p == 1:
                        for j in reversed(range(TOPK)):
                            rank = jnp.where(cur >= jnp.concatenate([top[j]] * (K // 8), axis=0), float(j), rank)
                if p == 1:
                    r2_ref[h, :, lanes] = rank.astype(r2_ref.dtype)
        a = [a_ref[i] for i in range(TOPK)]
        b = [b_ref[j] for j in range(TOPK)]
        row = [[a[i] + b[j] for j in range(TOPK // (i + 1))] for i in range(TOPK)]
        cand = [row[i][j] for (i, j) in pairs]
        if tie_break:
            sel = []
            for ip, cp in enumerate(cand):
                ahead = jnp.zeros(cp.shape, F32)
                for iq, cq in enumerate(cand):
                    if iq != ip:
                        ahead = ahead + jnp.where((cq >= cp) if iq < ip else (cq > cp), 1.0, 0.0)
                sel.append(ahead < float(TOPK))
        else:
            tau = _kth_pair_sum(row)
            sel = [c >= tau for c in cand]
        zsum = jnp.zeros(cand[0].shape, F32)
        cnt = [jnp.zeros(cand[0].shape, F32) for _ in range(TOPK)]
        for (i, j), cp, sp in zip(pairs, cand, sel):
            cnt[i] = cnt[i] + jnp.where(sp, 1.0, 0.0)
            zsum = zsum + jnp.where(sp, jnp.exp(cp - cand[0]), 0.0)
        for i in range(TOPK):
            cnt_ref[i] = cnt[i]
        iz_ref[...] = 0.5 / zsum
        if not tie_break:
            total = functools.reduce(jnp.add, cnt)
            bad8 = jnp.where(total != float(TOPK), 1.0, 0.0)
            bad = jnp.maximum(bad, jnp.max(bad8, axis=0, keepdims=True))
        for h in range(NH):
            s1 = s_ref[2 * h, :, lanes]
            n1 = jnp.zeros(s1.shape, F32)
            for i in range(TOPK):
                hit = (r1_ref[h] == float(i)) if tie_break else (s1 == a_ref[i, h:h + 1, :])
                n1 = jnp.where(hit, cnt_ref[i, h:h + 1, :], n1)
            n1_ref[h, :, lanes] = n1
            c1_ref[h, :, lanes] = jnp.exp(s1 - a_ref[0, h:h + 1, :]) * iz_ref[h:h + 1, :]
            s2 = s_ref[2 * h + 1, :, lanes]
            e2_ref[h, :, lanes] = jnp.exp(s2 - b_ref[0, h:h + 1, :]).astype(e2_ref.dtype)
        return bad

    def group(gi, carry):
        lanes = pl.ds(pl.multiple_of(gi * LANES, LANES), LANES)
        bad = select_stats(lanes, tie_break=False)

        @pl.when(jnp.max(bad) > 0.0)
        def _():
            select_stats(lanes, tie_break=True)

        return carry

    lax.fori_loop(0, tm // LANES, group, 0)


def _topk(xt, wq_t, keys, tm):
    D, T = xt.shape
    NH, K = PEER_HEADS, PEER_KEYS
    stat = pl.BlockSpec((NH, K, tm), lambda i: (0, 0, i))
    return pl.pallas_call(
        functools.partial(_topk_kernel, tm=tm),
        grid=(T // tm,),
        in_specs=[pl.BlockSpec((D, tm), lambda i: (0, i)),
                  pl.BlockSpec(wq_t.shape, lambda i: (0, 0)),
                  pl.BlockSpec(keys.shape, lambda i: (0, 0, 0))],
        out_specs=[stat] * 4,
        out_shape=[jax.ShapeDtypeStruct((NH, K, T), BF16), jax.ShapeDtypeStruct((NH, K, T), BF16),
                   jax.ShapeDtypeStruct((NH, K, T), F32), jax.ShapeDtypeStruct((NH, K, T), F32)],
        scratch_shapes=[pltpu.VMEM((wq_t.shape[0], tm), BF16),
                        pltpu.VMEM((2 * NH, K, tm), F32),
                        pltpu.VMEM((TOPK, NH, LANES), F32),
                        pltpu.VMEM((TOPK, NH, LANES), F32),
                        pltpu.VMEM((TOPK, NH, LANES), F32),
                        pltpu.VMEM((NH, LANES), F32),
                        pltpu.VMEM((NH, K, LANES), F32)],
        compiler_params=_params(("parallel",)),
        name="peer_topk",
    )(xt, wq_t, keys)


def _gelu2(x):
    return x + x * jnp.tanh(x * (GELU_C0 + GELU_C1 * (x * x)))


def _peer_kernel(xt_ref, u_ref, vt_ref, r2_ref, e2_ref, n1_ref, c1_ref, x1_ref, fg_ref, o_ref,
                 acc_ref, pt_ref, at_ref, *, te, tm, nsplit):
    NH, K = PEER_HEADS, PEER_KEYS
    e = pl.program_id(1)

    @pl.when(e == 0)
    def _():
        acc_ref[...] = jnp.zeros_like(acc_ref)

    n_i1 = te // K
    i1_rows = pl.ds(pl.multiple_of(e * n_i1, n_i1), n_i1)
    sub = te // nsplit

    def scores(s):
        at_ref[s] = jnp.dot(u_ref[s * sub:(s + 1) * sub, :], xt_ref[...], preferred_element_type=F32)

    def gated(s):
        nk = sub // K
        for lg in range(tm // LANES):
            lanes = slice(lg * LANES, (lg + 1) * LANES)
            gates = [None] * nk
            for h in range(NH):
                r2 = r2_ref[h, :, lanes]
                e2 = e2_ref[h, :, lanes]
                n1t = n1_ref[h, i1_rows, lanes]
                c1t = c1_ref[h, i1_rows, lanes]
                for k in range(nk):
                    kk = s * nk + k
                    n1 = jnp.broadcast_to(n1t[kk:kk + 1, :], (K, LANES)).astype(BF16)
                    c1 = jnp.broadcast_to(c1t[kk:kk + 1, :], (K, LANES)).astype(BF16)
                    term = jnp.minimum(jnp.maximum(n1 - r2, 0), e2) * c1
                    gates[k] = term if gates[k] is None else gates[k] + term
            for k in range(nk):
                kk = s * nk + k
                act = _gelu2(at_ref[s, k * K:(k + 1) * K, lanes]).astype(BF16)
                pt_ref[kk * K:(kk + 1) * K, lanes] = act * gates[k]

    def mixed(s0, ns):
        rows = slice(s0 * sub, (s0 + ns) * sub)
        for dm in range(0, acc_ref.shape[0], OUT_CHUNK):
            out = slice(dm, dm + OUT_CHUNK)
            acc_ref[out, :] += jnp.dot(vt_ref[out, rows], pt_ref[rows, :], preferred_element_type=F32)

    assert nsplit % MIX_GROUP == 0
    for s in range(nsplit):
        scores(s)
    for s in range(nsplit):
        gated(s)
        if s % MIX_GROUP == MIX_GROUP - 1:
            mixed(s - MIX_GROUP + 1, MIX_GROUP)

    @pl.when(e == pl.num_programs(1) - 1)
    def _():
        o_ref[...] = _rms(x1_ref[...] + acc_ref[...].T, fg_ref[...])


def _peer(xt, u, vt, r2, e2, n1, c1, x1, fg, tm, te, nsplit=8):
    D, T = xt.shape
    E = u.shape[0]
    NH, K = PEER_HEADS, PEER_KEYS
    assert te % (8 * K) == 0, "a step must cover whole sublane tiles of first-half keys"
    stat = pl.BlockSpec((NH, K, tm), lambda i, e: (0, 0, i))
    return pl.pallas_call(
        functools.partial(_peer_kernel, te=te, tm=tm, nsplit=nsplit),
        grid=(T // tm, E // te),
        in_specs=[pl.BlockSpec((D, tm), lambda i, e: (0, i)),
                  pl.BlockSpec((te, D), lambda i, e: (e, 0)),
                  pl.BlockSpec((D, te), lambda i, e: (0, e)),
                  stat, stat, stat, stat,
                  pl.BlockSpec((tm, D), lambda i, e: (i, 0)),
                  pl.BlockSpec((1, D), lambda i, e: (0, 0))],
        out_specs=pl.BlockSpec((tm, D), lambda i, e: (i, 0)),
        out_shape=jax.ShapeDtypeStruct((T, D), F32),
        scratch_shapes=[pltpu.VMEM((D, tm), F32), pltpu.VMEM((te, tm), BF16),
                        pltpu.VMEM((nsplit, te // nsplit, tm), F32)],
        compiler_params=_params(("parallel", "arbitrary")),
        name="peer_dense",
    )(xt, u, vt, r2, e2, n1, c1, x1, fg)


def _tiles(T, S):
    def pick(n, pref):
        t = min(pref, n)
        while n % t:
            t //= 2
        return t
    return dict(tm_in=pick(T, 1024), tn_in=1536, nck=pick(S // CHUNK, 16), tg=pick(T, 512),
                tm_out=pick(T, 512), tm_topk=pick(T, 512), tm_peer=pick(T, 512), te=16 * PEER_KEYS)


def _layer(x2d, B, S, norm1_g, w_in, conv_w, conv_b, b_igate, b_fgate, mlstm_norm_g,
           gmlp_ln_g, gmlp_ln_b, gmlp_w_s, gmlp_b_s, w_out, norm2_g,
           peer_w_query, peer_sub_keys, peer_u, peer_v, out_g):
    T, D = x2d.shape
    H, L, W = HEADS, CHUNK, MLSTM_W
    nc = S // L
    t = _tiles(T, S)

    g0 = 4 * W
    u0 = g0 + 2 * H
    w_main = jnp.concatenate([w_in[:, :g0], w_in[:, u0:]], axis=1).astype(BF16)
    w_gate = jnp.pad(w_in[:, g0:u0], ((0, 0), (0, LANES - 2 * H))).astype(BF16)
    z, zg = _inproj(x2d, norm1_g[None, :], w_main, w_gate, t["tm_in"], t["tn_in"])

    gcols = zg[:, :2 * H].reshape(B, nc, L, 2, H).transpose(3, 2, 0, 4, 1).reshape(2, L, B * H * nc)
    bi_row = jnp.broadcast_to(b_igate[None, :, None], (B, H, nc)).reshape(1, -1)
    bf_row = jnp.broadcast_to(b_fgate[None, :, None], (B, H, nc)).reshape(1, -1)
    cola, sc, emt, wg, r, decay = _gates(gcols[0], gcols[1], bi_row, bf_row, nc)
    cs = jnp.stack([cola, sc, emt, wg]).reshape(4, L, B, H, nc).transpose(2, 4, 1, 0, 3).reshape(B, nc, L, 4 * H)
    r_rows = jnp.pad(r.reshape(L, B, H, nc).transpose(1, 3, 2, 0), ((0, 0), (0, 0), (0, 0), (0, 2 * HEAD_DIM - L)))
    d_rows = jnp.broadcast_to(decay[0].reshape(B, H, nc).transpose(0, 2, 1)[..., None], (B, nc, H, 2 * HEAD_DIM))
    rs = jnp.concatenate([r_rows, d_rows], axis=2)
    ya = _mlstm(z, cs, rs, conv_w, conv_b[None, :], mlstm_norm_g[None, :], B, S, t["nck"])

    bs_b = jnp.broadcast_to(gmlp_b_s[:, :, None], (GROUPS, GMLP_CHUNK, GROUP_DIM))
    yb = _gmlp(z, gmlp_ln_g[None, :], gmlp_ln_b[None, :], gmlp_w_s, bs_b, t["tg"])

    w_o = w_out.astype(BF16)
    x1, xt = _outproj(x2d, ya, yb, w_o[:W], w_o[W:], norm2_g[None, :], t["tm_out"])

    wq_t = peer_w_query.T.astype(BF16)
    keys = peer_sub_keys.reshape(2 * PEER_HEADS, PEER_KEYS, PEER_HALF).astype(BF16)
    r2, e2, n1, c1 = _topk(xt, wq_t, keys, t["tm_topk"])
    return _peer(xt, peer_u.astype(BF16), peer_v.T.astype(BF16), r2, e2, n1, c1, x1,
                 out_g[None, :], t["tm_peer"], t["te"])


def kernel(x, norm1_g, w_in, conv_w, conv_b, b_igate, b_fgate, mlstm_norm_g, gmlp_ln_g, gmlp_ln_b,
           gmlp_w_s, gmlp_b_s, w_out, norm2_g, peer_w_query, peer_sub_keys, peer_u, peer_v, final_g):
    B, S, D = x.shape
    depth = norm1_g.shape[0]
    assert depth == 1, "final rmsnorm is fused into the last layer's PEER kernel"
    x2d = x.reshape(B * S, D)
    out = _layer(x2d, B, S, norm1_g[0], w_in[0], conv_w[0], conv_b[0], b_igate[0], b_fgate[0],
                 mlstm_norm_g[0], gmlp_ln_g[0], gmlp_ln_b[0], gmlp_w_s[0], gmlp_b_s[0], w_out[0],
                 norm2_g[0], peer_w_query[0], peer_sub_keys[0], peer_u[0], peer_v[0], final_g)
    return out.reshape(B, S, D)
```

```python
import functools
import math

import jax
import jax.numpy as jnp
from jax import lax
from jax.experimental import pallas as pl
from jax.experimental.pallas import tpu as pltpu

F32 = jnp.float32
BF16 = jnp.bfloat16
EPS = 1e-6

CHUNK = 64
HEADS = 4
HEAD_DIM = 128
MLSTM_W = HEADS * HEAD_DIM
CONV_K = 4
GROUPS = 4
GROUP_DIM = 128
GMLP_W = GROUPS * GROUP_DIM
GMLP_CHUNK = 128
PEER_HEADS = 8
PEER_KEYS = 128
PEER_HALF = 128
TOPK = 16
NO_RANK = 127.0
LANES = 128
BF16_ROWS = 16
OUT_CHUNK = 128
MIX_GROUP = 1
GELU_C0 = math.sqrt(2.0 / math.pi)
GELU_C1 = GELU_C0 * 0.044715

VMEM_LIMIT = 56 * 1024 * 1024


def _params(sem):
    return pltpu.CompilerParams(dimension_semantics=sem, vmem_limit_bytes=VMEM_LIMIT)


def _rms(x, g):
    return x * lax.rsqrt(jnp.mean(x * x, axis=-1, keepdims=True) + EPS) * g


def _inproj_kernel(x_ref, g_ref, w_ref, wg_ref, z_ref, gate_ref, xn_ref):
    j = pl.program_id(1)

    @pl.when(j == 0)
    def _():
        xn_ref[...] = _rms(x_ref[...], g_ref[...]).astype(BF16)

    z_ref[...] = jnp.dot(xn_ref[...], w_ref[...], preferred_element_type=F32).astype(z_ref.dtype)

    @pl.when(j == pl.num_programs(1) - 1)
    def _():
        gate_ref[...] = jnp.dot(xn_ref[...], wg_ref[...], preferred_element_type=F32)


def _inproj(x, g, w, wg, tm, tn):
    T, D = x.shape
    N = w.shape[1]
    return pl.pallas_call(
        _inproj_kernel,
        grid=(T // tm, N // tn),
        in_specs=[pl.BlockSpec((tm, D), lambda i, j: (i, 0)),
                  pl.BlockSpec((1, D), lambda i, j: (0, 0)),
                  pl.BlockSpec((D, tn), lambda i, j: (0, j)),
                  pl.BlockSpec(wg.shape, lambda i, j: (0, 0))],
        out_specs=[pl.BlockSpec((tm, tn), lambda i, j: (i, j)),
                   pl.BlockSpec((tm, wg.shape[1]), lambda i, j: (i, 0))],
        out_shape=[jax.ShapeDtypeStruct((T, N), BF16), jax.ShapeDtypeStruct((T, wg.shape[1]), F32)],
        scratch_shapes=[pltpu.VMEM((tm, D), BF16)],
        compiler_params=_params(("parallel", "arbitrary")),
        name="inproj",
    )(x, g, w, wg)


def _gate_kernel(i_ref, f_ref, bi_ref, bf_ref,
                 cola_ref, sc_ref, emt_ref, wg_ref, r_ref, decay_ref, *, nchunks):
    ig = i_ref[...] + bi_ref[...]
    lf = jax.nn.log_sigmoid(f_ref[...] + bf_ref[...])
    L, NC = ig.shape
    row = lax.broadcasted_iota(jnp.int32, (L, NC), 0)

    b = lf
    sh = 1
    while sh < L:
        b = b + jnp.where(row >= sh, pltpu.roll(b, sh, axis=0), 0.0)
        sh *= 2
    r = ig - b
    cm = r
    sh = 1
    while sh < L:
        cm = jnp.maximum(cm, jnp.where(row >= sh, pltpu.roll(cm, sh, axis=0), -jnp.inf))
        sh *= 2

    bl = jnp.broadcast_to(b[L - 1:L, :], (8, NC))
    gmax = bl + jnp.broadcast_to(cm[L - 1:L, :], (8, NC))
    lane = lax.broadcasted_iota(jnp.int32, (8, NC), 1)
    first = (lane % nchunks) == 0

    def step(_, m_new):
        m_prev = jnp.where(first, 0.0, pltpu.roll(m_new, 1, axis=1))
        return jnp.maximum(bl + m_prev, gmax)

    m_new = lax.fori_loop(0, nchunks, step, gmax)
    m_prev = jnp.where(first, 0.0, pltpu.roll(m_new, 1, axis=1))

    inter = b + m_prev[0:1, :]
    m_t = jnp.maximum(inter, b + cm)
    cola_ref[...] = b - m_t
    sc_ref[...] = jnp.exp(inter - m_t)
    emt_ref[...] = jnp.exp(-m_t)
    wg_ref[...] = jnp.exp(bl[0:1, :] + r - m_new[0:1, :])
    r_ref[...] = r
    decay_ref[...] = jnp.exp(bl + m_prev - m_new)


def _gates(icol, fcol, bi_row, bf_row, nchunks):
    L, NC = icol.shape
    full = pl.BlockSpec((L, NC), lambda: (0, 0))
    row = pl.BlockSpec((1, NC), lambda: (0, 0))
    row8 = pl.BlockSpec((8, NC), lambda: (0, 0))
    return pl.pallas_call(
        functools.partial(_gate_kernel, nchunks=nchunks),
        in_specs=[full, full, row, row],
        out_specs=[full] * 5 + [row8],
        out_shape=[jax.ShapeDtypeStruct((L, NC), F32)] * 5 + [jax.ShapeDtypeStruct((8, NC), F32)],
        name="mlstm_gates",
    )(icol, fcol, bi_row, bf_row)


def _mlstm_kernel(q_ref, k_ref, v_ref, o_ref, cs_ref, rs_ref, cw_ref, cb_ref, ng_ref, y_ref,
                  caug_ref, hq_ref, hk_ref, *, nck):
    L, H, Dh, W = CHUNK, HEADS, HEAD_DIM, MLSTM_W
    sblk = nck * L

    @pl.when(pl.program_id(1) == 0)
    def _():
        caug_ref[...] = jnp.zeros_like(caug_ref)
        hq_ref[...] = jnp.zeros_like(hq_ref)
        hk_ref[...] = jnp.zeros_like(hk_ref)

    causal = (lax.broadcasted_iota(jnp.int32, (L, L), 0) >= lax.broadcasted_iota(jnp.int32, (L, L), 1))
    lane0 = lax.broadcasted_iota(jnp.int32, (L, Dh), 1) == 0

    def conv_silu(x_ref, hist_ref, c, cw, cb):
        xc = x_ref[pl.ds(pl.multiple_of(c * L, L), L), :].astype(F32)
        start = pl.multiple_of(jnp.maximum(c * L - BF16_ROWS, 0), BF16_ROWS)
        prev = x_ref[pl.ds(start, BF16_ROWS), :].astype(F32)[BF16_ROWS - 8:, :]
        prev = jnp.where(c == 0, hist_ref[...], prev)
        ext = jnp.concatenate([prev, xc], axis=0)
        out = cb
        for j in range(CONV_K):
            sh = CONV_K - 1 - j
            xs = ext if sh == 0 else pltpu.roll(ext, sh, axis=0)
            out = out + xs[8:, :] * cw[j:j + 1, :]
        return out * jax.nn.sigmoid(out)

    def body(c, carry):
        rows = pl.ds(pl.multiple_of(c * L, L), L)
        q = conv_silu(q_ref, hq_ref, c, cw_ref[:, :W], cb_ref[:, :W])
        k = conv_silu(k_ref, hk_ref, c, cw_ref[:, W:], cb_ref[:, W:]) * (Dh ** -0.5)
        v = v_ref[rows, :].astype(F32)
        kt = k.T
        cs = cs_ref[c]
        rs = rs_ref[c]
        outs = []
        for h in range(H):
            sl = slice(h * Dh, (h + 1) * Dh)
            qh = q[:, sl].astype(BF16)
            kh = k[:, sl].astype(BF16)
            kth = kt[sl, :].astype(BF16)
            vh = v[:, sl]
            cola = cs[:, h:h + 1]
            sc = cs[:, H + h:H + h + 1]
            emt = cs[:, 2 * H + h:2 * H + h + 1]
            wg = cs[:, 3 * H + h:3 * H + h + 1]
            rowr = rs[h:h + 1, :L]
            decay = rs[H + h:H + h + 1, :]
            sqk = lax.dot_general(qh, kh, (((1,), (1,)), ((), ())), preferred_element_type=F32)
            w = jnp.exp(jnp.where(causal, cola + rowr, -jnp.inf))
            smat = sqk * w
            den_i = jnp.sum(smat, axis=-1, keepdims=True)
            caug = caug_ref[h]
            qc = jnp.dot(qh, caug.astype(BF16), preferred_element_type=F32)
            num = jnp.dot(smat.astype(BF16), vh.astype(BF16), preferred_element_type=F32) + sc * qc[:, :Dh]
            den = den_i + sc * qc[:, Dh:Dh + 1]
            hh = num / jnp.maximum(jnp.abs(den), emt)
            outs.append(hh * lax.rsqrt(jnp.mean(hh * hh, axis=-1, keepdims=True) + EPS))
            vaug = jnp.concatenate([wg * vh, jnp.where(lane0, wg, 0.0)], axis=1).astype(BF16)
            dc = jnp.dot(kth, vaug, preferred_element_type=F32)
            caug_ref[h] = decay * caug + dc
        hcat = jnp.concatenate(outs, axis=1)
        y = hcat * ng_ref[...] * jax.nn.sigmoid(o_ref[rows, :].astype(F32))
        y_ref[rows, :] = y.astype(y_ref.dtype)
        return carry

    lax.fori_loop(0, nck, body, 0)
    hq_ref[...] = q_ref[pl.ds(sblk - BF16_ROWS, BF16_ROWS), :].astype(F32)[BF16_ROWS - 8:, :]
    hk_ref[...] = k_ref[pl.ds(sblk - BF16_ROWS, BF16_ROWS), :].astype(F32)[BF16_ROWS - 8:, :]


def _mlstm(z, cs, rs, conv_w, conv_b, norm_g, B, S, nck):
    T = z.shape[0]
    L, H, Dh, W = CHUNK, HEADS, HEAD_DIM, MLSTM_W
    sblk = nck * L
    nsb = S // sblk

    def zspec(col):
        return pl.BlockSpec((sblk, W), lambda b, s, col=col: (b * nsb + s, col))

    return pl.pallas_call(
        functools.partial(_mlstm_kernel, nck=nck),
        grid=(B, nsb),
        in_specs=[zspec(0), zspec(1), zspec(2), zspec(3),
                  pl.BlockSpec((None, nck, L, 4 * H), lambda b, s: (b, s, 0, 0)),
                  pl.BlockSpec((None, nck, 2 * H, 2 * Dh), lambda b, s: (b, s, 0, 0)),
                  pl.BlockSpec((CONV_K, 2 * W), lambda b, s: (0, 0)),
                  pl.BlockSpec((1, 2 * W), lambda b, s: (0, 0)),
                  pl.BlockSpec((1, W), lambda b, s: (0, 0))],
        out_specs=pl.BlockSpec((sblk, W), lambda b, s: (b * nsb + s, 0)),
        out_shape=jax.ShapeDtypeStruct((T, W), BF16),
        scratch_shapes=[pltpu.VMEM((H, Dh, 2 * Dh), F32),
                        pltpu.VMEM((8, W), F32),
                        pltpu.VMEM((8, W), F32)],
        compiler_params=_params(("parallel", "arbitrary")),
        name="mlstm",
    )(z, z, z, z, cs, rs, conv_w, conv_b, norm_g)


def _gmlp_kernel(u_ref, v_ref, lg_ref, lb_ref, ws_ref, bs_ref, y_ref, *, nchunk):
    C, G, Dg = GMLP_CHUNK, GROUPS, GROUP_DIM
    pos_t = lax.broadcasted_iota(jnp.int32, (C, C), 0) // CHUNK
    pos_s = lax.broadcasted_iota(jnp.int32, (C, C), 1) // CHUNK
    mask = (pos_t >= pos_s).astype(F32)
    for g in range(G):
        wm = (ws_ref[g] * mask).astype(BF16)
        cols = slice(g * Dg, (g + 1) * Dg)
        for c in range(nchunk):
            rows = slice(c * C, (c + 1) * C)
            vv = jax.nn.gelu(v_ref[rows, cols].astype(F32))
            mu = jnp.mean(vv, axis=-1, keepdims=True)
            var = jnp.mean(jnp.square(vv - mu), axis=-1, keepdims=True)
            vv = (vv - mu) * lax.rsqrt(var + EPS)
            vv = vv * lg_ref[:, cols] + lb_ref[:, cols]
            mix = jnp.dot(wm, vv.astype(BF16), preferred_element_type=F32) + bs_ref[g]
            y_ref[rows, cols] = (jax.nn.gelu(u_ref[rows, cols].astype(F32)) * mix).astype(y_ref.dtype)


def _gmlp(z, ln_g, ln_b, w_s, bs_b, tg):
    T = z.shape[0]
    W, C, G = GMLP_W, GMLP_CHUNK, GROUPS
    return pl.pallas_call(
        functools.partial(_gmlp_kernel, nchunk=tg // C),
        grid=(T // tg,),
        in_specs=[pl.BlockSpec((tg, W), lambda i: (i, 4)),
                  pl.BlockSpec((tg, W), lambda i: (i, 5)),
                  pl.BlockSpec((1, W), lambda i: (0, 0)),
                  pl.BlockSpec((1, W), lambda i: (0, 0)),
                  pl.BlockSpec((G, C, C), lambda i: (0, 0, 0)),
                  pl.BlockSpec((G, C, GROUP_DIM), lambda i: (0, 0, 0))],
        out_specs=pl.BlockSpec((tg, W), lambda i: (i, 0)),
        out_shape=jax.ShapeDtypeStruct((T, W), BF16),
        compiler_params=_params(("parallel",)),
        name="gmlp",
    )(z, z, ln_g, ln_b, w_s, bs_b)


def _outproj_kernel(x_ref, ya_ref, yb_ref, wa_ref, wb_ref, g_ref, x1_ref, xt_ref):
    x1 = (x_ref[...]
          + jnp.dot(ya_ref[...], wa_ref[...], preferred_element_type=F32)
          + jnp.dot(yb_ref[...], wb_ref[...], preferred_element_type=F32))
    x1_ref[...] = x1
    xt_ref[...] = _rms(x1, g_ref[...]).T.astype(BF16)


def _outproj(x, ya, yb, wa, wb, g, tm):
    T, D = x.shape
    Wh = ya.shape[1]
    return pl.pallas_call(
        _outproj_kernel,
        grid=(T // tm,),
        in_specs=[pl.BlockSpec((tm, D), lambda i: (i, 0)),
                  pl.BlockSpec((tm, Wh), lambda i: (i, 0)),
                  pl.BlockSpec((tm, Wh), lambda i: (i, 0)),
                  pl.BlockSpec((Wh, D), lambda i: (0, 0)),
                  pl.BlockSpec((Wh, D), lambda i: (0, 0)),
                  pl.BlockSpec((1, D), lambda i: (0, 0))],
        out_specs=[pl.BlockSpec((tm, D), lambda i: (i, 0)),
                   pl.BlockSpec((D, tm), lambda i: (0, i))],
        out_shape=[jax.ShapeDtypeStruct((T, D), F32), jax.ShapeDtypeStruct((D, T), BF16)],
        compiler_params=_params(("parallel",)),
        name="outproj",
    )(x, ya, yb, wa, wb, g)


def _oddeven_mergesort(lo, hi):
    def merge(lo, hi, r):
        step = 2 * r
        if step < hi - lo:
            yield from merge(lo, hi, step)
            yield from merge(lo + r, hi, step)
            yield from ((i, i + r) for i in range(lo + r, hi - r, step))
        else:
            yield (lo, lo + r)
    if hi > lo:
        mid = lo + (hi - lo) // 2
        yield from _oddeven_mergesort(lo, mid)
        yield from _oddeven_mergesort(mid + 1, hi)
        yield from merge(lo, hi, 1)


SORT16 = tuple(_oddeven_mergesort(0, TOPK - 1))
BITONIC16 = tuple((i, i + d) for d in (8, 4, 2, 1) for i in range(TOPK) if not i & d)


def _exchange(v, net):
    v = list(v)
    for i, j in net:
        if v[i] is None:
            v[i], v[j] = v[j], None
        elif v[j] is not None:
            v[i], v[j] = jnp.maximum(v[i], v[j]), jnp.minimum(v[i], v[j])
    return v


def _merge_top16(a, b):
    def larger(x, y):
        return y if x is None else x if y is None else jnp.maximum(x, y)
    return _exchange([larger(a[r], b[TOPK - 1 - r]) for r in range(TOPK)], BITONIC16)


def _kth_pair_sum(row):
    pad = lambda run: run + [None] * (TOPK - len(run))
    ones = [row[i][0] for i in range(8, TOPK)]
    g0 = row[0]
    g1 = _exchange(row[1] + ones[::-1], BITONIC16)
    g2 = _exchange(row[2] + row[3] + row[4] + row[5] + row[6], SORT16)
    g3 = pad(row[7])
    return _merge_top16(_merge_top16(g0, g1), _merge_top16(g2, g3))[TOPK - 1]


def _top16_sorted(s):
    v = _exchange([s[8 * r:8 * r + 8, :] for r in range(TOPK)], SORT16)
    for shift in (4, 2, 1):
        v = _exchange([jnp.maximum(v[r], pltpu.roll(v[TOPK - 1 - r], shift, axis=0)) for r in range(TOPK)],
                      BITONIC16)
    return v


def _topk_kernel(xt_ref, wq_ref, keys_ref, r2_ref, e2_ref, n1_ref, c1_ref,
                 qt_ref, s_ref, a_ref, b_ref, cnt_ref, iz_ref, r1_ref, *, tm):
    NH, K = PEER_HEADS, PEER_KEYS
    qt_ref[...] = jnp.dot(wq_ref[...], xt_ref[...], preferred_element_type=F32).astype(BF16)
    for hp in range(2 * NH):
        s_ref[hp] = jnp.dot(keys_ref[hp], qt_ref[hp * PEER_HALF:(hp + 1) * PEER_HALF, :],
                            preferred_element_type=F32)

    pairs = [(i, j) for i in range(TOPK) for j in range(TOPK) if (i + 1) * (j + 1) <= TOPK]
    key_id = lax.broadcasted_iota(jnp.int32, (K, LANES), 0).astype(F32)

    def select_stats(lanes, tie_break):
        bad = jnp.zeros((1, LANES), F32)
        for h in range(NH):
            for p in range(2):
                cur = s_ref[2 * h + p, :, lanes]
                top_ref = a_ref if p == 0 else b_ref
                rank = jnp.full(cur.shape, NO_RANK, F32)
                if tie_break:
                    for j in range(TOPK):
                        m = jnp.max(cur, axis=0, keepdims=True)
                        first = jnp.min(jnp.where(cur == m, key_id, float(K)), axis=0, keepdims=True)
                        hit = key_id == first
                        top_ref[j, h:h + 1, :] = m
                        rank = jnp.where(hit, float(j), rank)
                        cur = jnp.where(hit, -jnp.inf, cur)
                    if p == 0:
                        r1_ref[h] = rank
                else:
                    top = _top16_sorted(cur)
                    for j in range(TOPK):
                        top_ref[j, h:h + 1, :] = top[j][0:1, :]
                    dup = functools.reduce(jnp.logical_or, [top[j] == top[j + 1] for j in range(TOPK - 1)])
                    reach = functools.reduce(jnp.add, [jnp.where(cur[8 * r:8 * r + 8, :] >= top[TOPK - 1], 1.0, 0.0)
                                                       for r in range(K // 8)])
                    reach = jnp.sum(reach, axis=0, keepdims=True)
                    tied = jnp.logical_or(dup[0:1, :], reach != float(TOPK))
                    bad = jnp.maximum(bad, jnp.where(tied, 1.0, 0.0))
                    if p == 1:
                        for j in reversed(range(TOPK)):
                            rank = jnp.where(cur >= jnp.concatenate([top[j]] * (K // 8), axis=0), float(j), rank)
                if p == 1:
                    r2_ref[h, :, lanes] = rank.astype(r2_ref.dtype)
        a = [a_ref[i] for i in range(TOPK)]
        b = [b_ref[j] for j in range(TOPK)]
        row = [[a[i] + b[j] for j in range(TOPK // (i + 1))] for i in range(TOPK)]
        cand = [row[i][j] for (i, j) in pairs]
        if tie_break:
            sel = []
            for ip, cp in enumerate(cand):
                ahead = jnp.zeros(cp.shape, F32)
                for iq, cq in enumerate(cand):
                    if iq != ip:
                        ahead = ahead + jnp.where((cq >= cp) if iq < ip else (cq > cp), 1.0, 0.0)
                sel.append(ahead < float(TOPK))
        else:
            tau = _kth_pair_sum(row)
            sel = [c >= tau for c in cand]
        zsum = jnp.zeros(cand[0].shape, F32)
        cnt = [jnp.zeros(cand[0].shape, F32) for _ in range(TOPK)]
        for (i, j), cp, sp in zip(pairs, cand, sel):
            cnt[i] = cnt[i] + jnp.where(sp, 1.0, 0.0)
            zsum = zsum + jnp.where(sp, jnp.exp(cp - cand[0]), 0.0)
        for i in range(TOPK):
            cnt_ref[i] = cnt[i]
        iz_ref[...] = 0.5 / zsum
        if not tie_break:
            total = functools.reduce(jnp.add, cnt)
            bad8 = jnp.where(total != float(TOPK), 1.0, 0.0)
            bad = jnp.maximum(bad, jnp.max(bad8, axis=0, keepdims=True))
        for h in range(NH):
            s1 = s_ref[2 * h, :, lanes]
            n1 = jnp.zeros(s1.shape, F32)
            for i in range(TOPK):
                hit = (r1_ref[h] == float(i)) if tie_break else (s1 == a_ref[i, h:h + 1, :])
                n1 = jnp.where(hit, cnt_ref[i, h:h + 1, :], n1)
            n1_ref[h, :, lanes] = n1
            c1_ref[h, :, lanes] = jnp.exp(s1 - a_ref[0, h:h + 1, :]) * iz_ref[h:h + 1, :]
            s2 = s_ref[2 * h + 1, :, lanes]
            e2_ref[h, :, lanes] = jnp.exp(s2 - b_ref[0, h:h + 1, :]).astype(e2_ref.dtype)
        return bad

    def group(gi, carry):
        lanes = pl.ds(pl.multiple_of(gi * LANES, LANES), LANES)
        bad = select_stats(lanes, tie_break=False)

        @pl.when(jnp.max(bad) > 0.0)
        def _():
            select_stats(lanes, tie_break=True)

        return carry

    lax.fori_loop(0, tm // LANES, group, 0)


def _topk(xt, wq_t, keys, tm):
    D, T = xt.shape
    NH, K = PEER_HEADS, PEER_KEYS
    stat = pl.BlockSpec((NH, K, tm), lambda i: (0, 0, i))
    return pl.pallas_call(
        functools.partial(_topk_kernel, tm=tm),
        grid=(T // tm,),
        in_specs=[pl.BlockSpec((D, tm), lambda i: (0, i)),
                  pl.BlockSpec(wq_t.shape, lambda i: (0, 0)),
                  pl.BlockSpec(keys.shape, lambda i: (0, 0, 0))],
        out_specs=[stat] * 4,
        out_shape=[jax.ShapeDtypeStruct((NH, K, T), BF16), jax.ShapeDtypeStruct((NH, K, T), BF16),
                   jax.ShapeDtypeStruct((NH, K, T), F32), jax.ShapeDtypeStruct((NH, K, T), F32)],
        scratch_shapes=[pltpu.VMEM((wq_t.shape[0], tm), BF16),
                        pltpu.VMEM((2 * NH, K, tm), F32),
                        pltpu.VMEM((TOPK, NH, LANES), F32),
                        pltpu.VMEM((TOPK, NH, LANES), F32),
                        pltpu.VMEM((TOPK, NH, LANES), F32),
                        pltpu.VMEM((NH, LANES), F32),
                        pltpu.VMEM((NH, K, LANES), F32)],
        compiler_params=_params(("parallel",)),
        name="peer_topk",
    )(xt, wq_t, keys)


def _gelu2(x):
    return x + x * jnp.tanh(x * (GELU_C0 + GELU_C1 * (x * x)))


def _peer_kernel(xt_ref, u_ref, vt_ref, r2_ref, e2_ref, n1_ref, c1_ref, x1_ref, fg_ref, o_ref,
                 acc_ref, pt_ref, at_ref, *, te, tm, nsplit):
    NH, K = PEER_HEADS, PEER_KEYS
    e = pl.program_id(1)

    @pl.when(e == 0)
    def _():
        acc_ref[...] = jnp.zeros_like(acc_ref)

    n_i1 = te // K
    i1_rows = pl.ds(pl.multiple_of(e * n_i1, n_i1), n_i1)
    sub = te // nsplit

    def scores(s):
        at_ref[s] = jnp.dot(u_ref[s * sub:(s + 1) * sub, :], xt_ref[...], preferred_element_type=F32)

    def gated(s):
        nk = sub // K
        for lg in range(tm // LANES):
            lanes = slice(lg * LANES, (lg + 1) * LANES)
            gates = [None] * nk
            for h in range(NH):
                r2 = r2_ref[h, :, lanes]
                e2 = e2_ref[h, :, lanes]
                n1t = n1_ref[h, i1_rows, lanes]
                c1t = c1_ref[h, i1_rows, lanes]
                for k in range(nk):
                    kk = s * nk + k
                    n1 = jnp.broadcast_to(n1t[kk:kk + 1, :], (K, LANES)).astype(BF16)
                    c1 = jnp.broadcast_to(c1t[kk:kk + 1, :], (K, LANES)).astype(BF16)
                    term = jnp.minimum(jnp.maximum(n1 - r2, 0), e2) * c1
                    gates[k] = term if gates[k] is None else gates[k] + term
            for k in range(nk):
                kk = s * nk + k
                act = _gelu2(at_ref[s, k * K:(k + 1) * K, lanes]).astype(BF16)
                pt_ref[kk * K:(kk + 1) * K, lanes] = act * gates[k]

    def mixed(s0, ns):
        rows = slice(s0 * sub, (s0 + ns) * sub)
        for dm in range(0, acc_ref.shape[0], OUT_CHUNK):
            out = slice(dm, dm + OUT_CHUNK)
            acc_ref[out, :] += jnp.dot(vt_ref[out, rows], pt_ref[rows, :], preferred_element_type=F32)

    assert nsplit % MIX_GROUP == 0
    for s in range(nsplit):
        scores(s)
    for s in range(nsplit):
        gated(s)
        if s % MIX_GROUP == MIX_GROUP - 1:
            mixed(s - MIX_GROUP + 1, MIX_GROUP)

    @pl.when(e == pl.num_programs(1) - 1)
    def _():
        o_ref[...] = _rms(x1_ref[...] + acc_ref[...].T, fg_ref[...])


def _peer(xt, u, vt, r2, e2, n1, c1, x1, fg, tm, te, nsplit=8):
    D, T = xt.shape
    E = u.shape[0]
    NH, K = PEER_HEADS, PEER_KEYS
    assert te % (8 * K) == 0, "a step must cover whole sublane tiles of first-half keys"
    stat = pl.BlockSpec((NH, K, tm), lambda i, e: (0, 0, i))
    return pl.pallas_call(
        functools.partial(_peer_kernel, te=te, tm=tm, nsplit=nsplit),
        grid=(T // tm, E // te),
        in_specs=[pl.BlockSpec((D, tm), lambda i, e: (0, i)),
                  pl.BlockSpec((te, D), lambda i, e: (e, 0)),
                  pl.BlockSpec((D, te), lambda i, e: (0, e)),
                  stat, stat, stat, stat,
                  pl.BlockSpec((tm, D), lambda i, e: (i, 0)),
                  pl.BlockSpec((1, D), lambda i, e: (0, 0))],
        out_specs=pl.BlockSpec((tm, D), lambda i, e: (i, 0)),
        out_shape=jax.ShapeDtypeStruct((T, D), F32),
        scratch_shapes=[pltpu.VMEM((D, tm), F32), pltpu.VMEM((te, tm), BF16),
                        pltpu.VMEM((nsplit, te // nsplit, tm), F32)],
        compiler_params=_params(("parallel", "arbitrary")),
        name="peer_dense",
    )(xt, u, vt, r2, e2, n1, c1, x1, fg)


def _tiles(T, S):
    def pick(n, pref):
        t = min(pref, n)
        while n % t:
            t //= 2
        return t
    return dict(tm_in=pick(T, 1024), tn_in=1536, nck=pick(S // CHUNK, 16), tg=pick(T, 512),
                tm_out=pick(T, 512), tm_topk=pick(T, 512), tm_peer=pick(T, 512), te=16 * PEER_KEYS)


def _layer(x2d, B, S, norm1_g, w_in, conv_w, conv_b, b_igate, b_fgate, mlstm_norm_g,
           gmlp_ln_g, gmlp_ln_b, gmlp_w_s, gmlp_b_s, w_out, norm2_g,
           peer_w_query, peer_sub_keys, peer_u, peer_v, out_g):
    T, D = x2d.shape
    H, L, W = HEADS, CHUNK, MLSTM_W
    nc = S // L
    t = _tiles(T, S)

    g0 = 4 * W
    u0 = g0 + 2 * H
    w_main = jnp.concatenate([w_in[:, :g0], w_in[:, u0:]], axis=1).astype(BF16)
    w_gate = jnp.pad(w_in[:, g0:u0], ((0, 0), (0, LANES - 2 * H))).astype(BF16)
    z, zg = _inproj(x2d, norm1_g[None, :], w_main, w_gate, t["tm_in"], t["tn_in"])

    gcols = zg[:, :2 * H].reshape(B, nc, L, 2, H).transpose(3, 2, 0, 4, 1).reshape(2, L, B * H * nc)
    bi_row = jnp.broadcast_to(b_igate[None, :, None], (B, H, nc)).reshape(1, -1)
    bf_row = jnp.broadcast_to(b_fgate[None, :, None], (B, H, nc)).reshape(1, -1)
    cola, sc, emt, wg, r, decay = _gates(gcols[0], gcols[1], bi_row, bf_row, nc)
    cs = jnp.stack([cola, sc, emt, wg]).reshape(4, L, B, H, nc).transpose(2, 4, 1, 0, 3).reshape(B, nc, L, 4 * H)
    r_rows = jnp.pad(r.reshape(L, B, H, nc).transpose(1, 3, 2, 0), ((0, 0), (0, 0), (0, 0), (0, 2 * HEAD_DIM - L)))
    d_rows = jnp.broadcast_to(decay[0].reshape(B, H, nc).transpose(0, 2, 1)[..., None], (B, nc, H, 2 * HEAD_DIM))
    rs = jnp.concatenate([r_rows, d_rows], axis=2)
    ya = _mlstm(z, cs, rs, conv_w, conv_b[None, :], mlstm_norm_g[None, :], B, S, t["nck"])

    bs_b = jnp.broadcast_to(gmlp_b_s[:, :, None], (GROUPS, GMLP_CHUNK, GROUP_DIM))
    yb = _gmlp(z, gmlp_ln_g[None, :], gmlp_ln_b[None, :], gmlp_w_s, bs_b, t["tg"])

    w_o = w_out.astype(BF16)
    x1, xt = _outproj(x2d, ya, yb, w_o[:W], w_o[W:], norm2_g[None, :], t["tm_out"])

    wq_t = peer_w_query.T.astype(BF16)
    keys = peer_sub_keys.reshape(2 * PEER_HEADS, PEER_KEYS, PEER_HALF).astype(BF16)
    r2, e2, n1, c1 = _topk(xt, wq_t, keys, t["tm_topk"])
    return _peer(xt, peer_u.astype(BF16), peer_v.T.astype(BF16), r2, e2, n1, c1, x1,
                 out_g[None, :], t["tm_peer"], t["te"])


def kernel(x, norm1_g, w_in, conv_w, conv_b, b_igate, b_fgate, mlstm_norm_g, gmlp_ln_g, gmlp_ln_b,
           gmlp_w_s, gmlp_b_s, w_out, norm2_g, peer_w_query, peer_sub_keys, peer_u, peer_v, final_g):
    B, S, D = x.shape
    depth = norm1_g.shape[0]
    assert depth == 1, "final rmsnorm is fused into the last layer's PEER kernel"
    x2d = x.reshape(B * S, D)
    out = _layer(x2d, B, S, norm1_g[0], w_in[0], conv_w[0], conv_b[0], b_igate[0], b_fgate[0],
                 mlstm_norm_g[0], gmlp_ln_g[0], gmlp_ln_b[0], gmlp_w_s[0], gmlp_b_s[0], w_out[0],
                 norm2_g[0], peer_w_query[0], peer_sub_keys[0], peer_u[0], peer_v[0], final_g)
    return out.reshape(B, S, D)
```

```python
import functools
import math

import jax
import jax.numpy as jnp
from jax import lax
from jax.experimental import pallas as pl
from jax.experimental.pallas import tpu as pltpu

F32 = jnp.float32
BF16 = jnp.bfloat16
EPS = 1e-6

CHUNK = 64
HEADS = 4
HEAD_DIM = 128
MLSTM_W = HEADS * HEAD_DIM
CONV_K = 4
GROUPS = 4
GROUP_DIM = 128
GMLP_W = GROUPS * GROUP_DIM
GMLP_CHUNK = 128
PEER_HEADS = 8
PEER_KEYS = 128
PEER_HALF = 128
TOPK = 16
NO_RANK = 127.0
LANES = 128
BF16_ROWS = 16
OUT_CHUNK = 256
MIX_GROUP = 1
GELU_C0 = math.sqrt(2.0 / math.pi)
GELU_C1 = GELU_C0 * 0.044715

VMEM_LIMIT = 56 * 1024 * 1024


def _params(sem):
    return pltpu.CompilerParams(dimension_semantics=sem, vmem_limit_bytes=VMEM_LIMIT)


def _rms(x, g):
    return x * lax.rsqrt(jnp.mean(x * x, axis=-1, keepdims=True) + EPS) * g


def _inproj_kernel(x_ref, g_ref, w_ref, wg_ref, z_ref, gate_ref, xn_ref):
    j = pl.program_id(1)

    @pl.when(j == 0)
    def _():
        xn_ref[...] = _rms(x_ref[...], g_ref[...]).astype(BF16)

    z_ref[...] = jnp.dot(xn_ref[...], w_ref[...], preferred_element_type=F32).astype(z_ref.dtype)

    @pl.when(j == pl.num_programs(1) - 1)
    def _():
        gate_ref[...] = jnp.dot(xn_ref[...], wg_ref[...], preferred_element_type=F32)


def _inproj(x, g, w, wg, tm, tn):
    T, D = x.shape
    N = w.shape[1]
    return pl.pallas_call(
        _inproj_kernel,
        grid=(T // tm, N // tn),
        in_specs=[pl.BlockSpec((tm, D), lambda i, j: (i, 0)),
                  pl.BlockSpec((1, D), lambda i, j: (0, 0)),
                  pl.BlockSpec((D, tn), lambda i, j: (0, j)),
                  pl.BlockSpec(wg.shape, lambda i, j: (0, 0))],
        out_specs=[pl.BlockSpec((tm, tn), lambda i, j: (i, j)),
                   pl.BlockSpec((tm, wg.shape[1]), lambda i, j: (i, 0))],
        out_shape=[jax.ShapeDtypeStruct((T, N), BF16), jax.ShapeDtypeStruct((T, wg.shape[1]), F32)],
        scratch_shapes=[pltpu.VMEM((tm, D), BF16)],
        compiler_params=_params(("parallel", "arbitrary")),
        name="inproj",
    )(x, g, w, wg)


def _gate_kernel(i_ref, f_ref, bi_ref, bf_ref,
                 cola_ref, sc_ref, emt_ref, wg_ref, r_ref, decay_ref, *, nchunks):
    ig = i_ref[...] + bi_ref[...]
    lf = jax.nn.log_sigmoid(f_ref[...] + bf_ref[...])
    L, NC = ig.shape
    row = lax.broadcasted_iota(jnp.int32, (L, NC), 0)

    b = lf
    sh = 1
    while sh < L:
        b = b + jnp.where(row >= sh, pltpu.roll(b, sh, axis=0), 0.0)
        sh *= 2
    r = ig - b
    cm = r
    sh = 1
    while sh < L:
        cm = jnp.maximum(cm, jnp.where(row >= sh, pltpu.roll(cm, sh, axis=0), -jnp.inf))
        sh *= 2

    bl = jnp.broadcast_to(b[L - 1:L, :], (8, NC))
    gmax = bl + jnp.broadcast_to(cm[L - 1:L, :], (8, NC))
    lane = lax.broadcasted_iota(jnp.int32, (8, NC), 1)
    first = (lane % nchunks) == 0

    def step(_, m_new):
        m_prev = jnp.where(first, 0.0, pltpu.roll(m_new, 1, axis=1))
        return jnp.maximum(bl + m_prev, gmax)

    m_new = lax.fori_loop(0, nchunks, step, gmax)
    m_prev = jnp.where(first, 0.0, pltpu.roll(m_new, 1, axis=1))

    inter = b + m_prev[0:1, :]
    m_t = jnp.maximum(inter, b + cm)
    cola_ref[...] = b - m_t
    sc_ref[...] = jnp.exp(inter - m_t)
    emt_ref[...] = jnp.exp(-m_t)
    wg_ref[...] = jnp.exp(bl[0:1, :] + r - m_new[0:1, :])
    r_ref[...] = r
    decay_ref[...] = jnp.exp(bl + m_prev - m_new)


def _gates(icol, fcol, bi_row, bf_row, nchunks):
    L, NC = icol.shape
    full = pl.BlockSpec((L, NC), lambda: (0, 0))
    row = pl.BlockSpec((1, NC), lambda: (0, 0))
    row8 = pl.BlockSpec((8, NC), lambda: (0, 0))
    return pl.pallas_call(
        functools.partial(_gate_kernel, nchunks=nchunks),
        in_specs=[full, full, row, row],
        out_specs=[full] * 5 + [row8],
        out_shape=[jax.ShapeDtypeStruct((L, NC), F32)] * 5 + [jax.ShapeDtypeStruct((8, NC), F32)],
        name="mlstm_gates",
    )(icol, fcol, bi_row, bf_row)


def _mlstm_kernel(q_ref, k_ref, v_ref, o_ref, cs_ref, rs_ref, cw_ref, cb_ref, ng_ref, y_ref,
                  caug_ref, hq_ref, hk_ref, *, nck):
    L, H, Dh, W = CHUNK, HEADS, HEAD_DIM, MLSTM_W
    sblk = nck * L

    @pl.when(pl.program_id(1) == 0)
    def _():
        caug_ref[...] = jnp.zeros_like(caug_ref)
        hq_ref[...] = jnp.zeros_like(hq_ref)
        hk_ref[...] = jnp.zeros_like(hk_ref)

    causal = (lax.broadcasted_iota(jnp.int32, (L, L), 0) >= lax.broadcasted_iota(jnp.int32, (L, L), 1))
    lane0 = lax.broadcasted_iota(jnp.int32, (L, Dh), 1) == 0

    def conv_silu(x_ref, hist_ref, c, cw, cb):
        xc = x_ref[pl.ds(pl.multiple_of(c * L, L), L), :].astype(F32)
        start = pl.multiple_of(jnp.maximum(c * L - BF16_ROWS, 0), BF16_ROWS)
        prev = x_ref[pl.ds(start, BF16_ROWS), :].astype(F32)[BF16_ROWS - 8:, :]
        prev = jnp.where(c == 0, hist_ref[...], prev)
        ext = jnp.concatenate([prev, xc], axis=0)
        out = cb
        for j in range(CONV_K):
            sh = CONV_K - 1 - j
            xs = ext if sh == 0 else pltpu.roll(ext, sh, axis=0)
            out = out + xs[8:, :] * cw[j:j + 1, :]
        return out * jax.nn.sigmoid(out)

    def body(c, carry):
        rows = pl.ds(pl.multiple_of(c * L, L), L)
        q = conv_silu(q_ref, hq_ref, c, cw_ref[:, :W], cb_ref[:, :W])
        k = conv_silu(k_ref, hk_ref, c, cw_ref[:, W:], cb_ref[:, W:]) * (Dh ** -0.5)
        v = v_ref[rows, :].astype(F32)
        kt = k.T
        cs = cs_ref[c]
        rs = rs_ref[c]
        outs = []
        for h in range(H):
            sl = slice(h * Dh, (h + 1) * Dh)
            qh = q[:, sl].astype(BF16)
            kh = k[:, sl].astype(BF16)
            kth = kt[sl, :].astype(BF16)
            vh = v[:, sl]
            cola = cs[:, h:h + 1]
            sc = cs[:, H + h:H + h + 1]
            emt = cs[:, 2 * H + h:2 * H + h + 1]
            wg = cs[:, 3 * H + h:3 * H + h + 1]
            rowr = rs[h:h + 1, :L]
            decay = rs[H + h:H + h + 1, :]
            sqk = lax.dot_general(qh, kh, (((1,), (1,)), ((), ())), preferred_element_type=F32)
            w = jnp.exp(jnp.where(causal, cola + rowr, -jnp.inf))
            smat = sqk * w
            den_i = jnp.sum(smat, axis=-1, keepdims=True)
            caug = caug_ref[h]
            qc = jnp.dot(qh, caug.astype(BF16), preferred_element_type=F32)
            num = jnp.dot(smat.astype(BF16), vh.astype(BF16), preferred_element_type=F32) + sc * qc[:, :Dh]
            den = den_i + sc * qc[:, Dh:Dh + 1]
            hh = num / jnp.maximum(jnp.abs(den), emt)
            outs.append(hh * lax.rsqrt(jnp.mean(hh * hh, axis=-1, keepdims=True) + EPS))
            vaug = jnp.concatenate([wg * vh, jnp.where(lane0, wg, 0.0)], axis=1).astype(BF16)
            dc = jnp.dot(kth, vaug, preferred_element_type=F32)
            caug_ref[h] = decay * caug + dc
        hcat = jnp.concatenate(outs, axis=1)
        y = hcat * ng_ref[...] * jax.nn.sigmoid(o_ref[rows, :].astype(F32))
        y_ref[rows, :] = y.astype(y_ref.dtype)
        return carry

    lax.fori_loop(0, nck, body, 0)
    hq_ref[...] = q_ref[pl.ds(sblk - BF16_ROWS, BF16_ROWS), :].astype(F32)[BF16_ROWS - 8:, :]
    hk_ref[...] = k_ref[pl.ds(sblk - BF16_ROWS, BF16_ROWS), :].astype(F32)[BF16_ROWS - 8:, :]


def _mlstm(z, cs, rs, conv_w, conv_b, norm_g, B, S, nck):
    T = z.shape[0]
    L, H, Dh, W = CHUNK, HEADS, HEAD_DIM, MLSTM_W
    sblk = nck * L
    nsb = S // sblk

    def zspec(col):
        return pl.BlockSpec((sblk, W), lambda b, s, col=col: (b * nsb + s, col))

    return pl.pallas_call(
        functools.partial(_mlstm_kernel, nck=nck),
        grid=(B, nsb),
        in_specs=[zspec(0), zspec(1), zspec(2), zspec(3),
                  pl.BlockSpec((None, nck, L, 4 * H), lambda b, s: (b, s, 0, 0)),
                  pl.BlockSpec((None, nck, 2 * H, 2 * Dh), lambda b, s: (b, s, 0, 0)),
                  pl.BlockSpec((CONV_K, 2 * W), lambda b, s: (0, 0)),
                  pl.BlockSpec((1, 2 * W), lambda b, s: (0, 0)),
                  pl.BlockSpec((1, W), lambda b, s: (0, 0))],
        out_specs=pl.BlockSpec((sblk, W), lambda b, s: (b * nsb + s, 0)),
        out_shape=jax.ShapeDtypeStruct((T, W), BF16),
        scratch_shapes=[pltpu.VMEM((H, Dh, 2 * Dh), F32),
                        pltpu.VMEM((8, W), F32),
                        pltpu.VMEM((8, W), F32)],
        compiler_params=_params(("parallel", "arbitrary")),
        name="mlstm",
    )(z, z, z, z, cs, rs, conv_w, conv_b, norm_g)


def _gmlp_kernel(u_ref, v_ref, lg_ref, lb_ref, ws_ref, bs_ref, y_ref, *, nchunk):
    C, G, Dg = GMLP_CHUNK, GROUPS, GROUP_DIM
    pos_t = lax.broadcasted_iota(jnp.int32, (C, C), 0) // CHUNK
    pos_s = lax.broadcasted_iota(jnp.int32, (C, C), 1) // CHUNK
    mask = (pos_t >= pos_s).astype(F32)
    for g in range(G):
        wm = (ws_ref[g] * mask).astype(BF16)
        cols = slice(g * Dg, (g + 1) * Dg)
        for c in range(nchunk):
            rows = slice(c * C, (c + 1) * C)
            vv = jax.nn.gelu(v_ref[rows, cols].astype(F32))
            mu = jnp.mean(vv, axis=-1, keepdims=True)
            var = jnp.mean(jnp.square(vv - mu), axis=-1, keepdims=True)
            vv = (vv - mu) * lax.rsqrt(var + EPS)
            vv = vv * lg_ref[:, cols] + lb_ref[:, cols]
            mix = jnp.dot(wm, vv.astype(BF16), preferred_element_type=F32) + bs_ref[g]
            y_ref[rows, cols] = (jax.nn.gelu(u_ref[rows, cols].astype(F32)) * mix).astype(y_ref.dtype)


def _gmlp(z, ln_g, ln_b, w_s, bs_b, tg):
    T = z.shape[0]
    W, C, G = GMLP_W, GMLP_CHUNK, GROUPS
    return pl.pallas_call(
        functools.partial(_gmlp_kernel, nchunk=tg // C),
        grid=(T // tg,),
        in_specs=[pl.BlockSpec((tg, W), lambda i: (i, 4)),
                  pl.BlockSpec((tg, W), lambda i: (i, 5)),
                  pl.BlockSpec((1, W), lambda i: (0, 0)),
                  pl.BlockSpec((1, W), lambda i: (0, 0)),
                  pl.BlockSpec((G, C, C), lambda i: (0, 0, 0)),
                  pl.BlockSpec((G, C, GROUP_DIM), lambda i: (0, 0, 0))],
        out_specs=pl.BlockSpec((tg, W), lambda i: (i, 0)),
        out_shape=jax.ShapeDtypeStruct((T, W), BF16),
        compiler_params=_params(("parallel",)),
        name="gmlp",
    )(z, z, ln_g, ln_b, w_s, bs_b)


def _outproj_kernel(x_ref, ya_ref, yb_ref, wa_ref, wb_ref, g_ref, x1_ref, xt_ref):
    x1 = (x_ref[...]
          + jnp.dot(ya_ref[...], wa_ref[...], preferred_element_type=F32)
          + jnp.dot(yb_ref[...], wb_ref[...], preferred_element_type=F32))
    x1_ref[...] = x1
    xt_ref[...] = _rms(x1, g_ref[...]).T.astype(BF16)


def _outproj(x, ya, yb, wa, wb, g, tm):
    T, D = x.shape
    Wh = ya.shape[1]
    return pl.pallas_call(
        _outproj_kernel,
        grid=(T // tm,),
        in_specs=[pl.BlockSpec((tm, D), lambda i: (i, 0)),
                  pl.BlockSpec((tm, Wh), lambda i: (i, 0)),
                  pl.BlockSpec((tm, Wh), lambda i: (i, 0)),
                  pl.BlockSpec((Wh, D), lambda i: (0, 0)),
                  pl.BlockSpec((Wh, D), lambda i: (0, 0)),
                  pl.BlockSpec((1, D), lambda i: (0, 0))],
        out_specs=[pl.BlockSpec((tm, D), lambda i: (i, 0)),
                   pl.BlockSpec((D, tm), lambda i: (0, i))],
        out_shape=[jax.ShapeDtypeStruct((T, D), F32), jax.ShapeDtypeStruct((D, T), BF16)],
        compiler_params=_params(("parallel",)),
        name="outproj",
    )(x, ya, yb, wa, wb, g)


def _oddeven_mergesort(lo, hi):
    def merge(lo, hi, r):
        step = 2 * r
        if step < hi - lo:
            yield from merge(lo, hi, step)
            yield from merge(lo + r, hi, step)
            yield from ((i, i + r) for i in range(lo + r, hi - r, step))
        else:
            yield (lo, lo + r)
    if hi > lo:
        mid = lo + (hi - lo) // 2
        yield from _oddeven_mergesort(lo, mid)
        yield from _oddeven_mergesort(mid + 1, hi)
        yield from merge(lo, hi, 1)


SORT16 = tuple(_oddeven_mergesort(0, TOPK - 1))
BITONIC16 = tuple((i, i + d) for d in (8, 4, 2, 1) for i in range(TOPK) if not i & d)


def _exchange(v, net):
    v = list(v)
    for i, j in net:
        if v[i] is None:
            v[i], v[j] = v[j], None
        elif v[j] is not None:
            v[i], v[j] = jnp.maximum(v[i], v[j]), jnp.minimum(v[i], v[j])
    return v


def _merge_top16(a, b):
    def larger(x, y):
        return y if x is None else x if y is None else jnp.maximum(x, y)
    return _exchange([larger(a[r], b[TOPK - 1 - r]) for r in range(TOPK)], BITONIC16)


def _kth_pair_sum(row):
    pad = lambda run: run + [None] * (TOPK - len(run))
    ones = [row[i][0] for i in range(8, TOPK)]
    g0 = row[0]
    g1 = _exchange(row[1] + ones[::-1], BITONIC16)
    g2 = _exchange(row[2] + row[3] + row[4] + row[5] + row[6], SORT16)
    g3 = pad(row[7])
    return _merge_top16(_merge_top16(g0, g1), _merge_top16(g2, g3))[TOPK - 1]


def _top16_sorted(s):
    v = _exchange([s[8 * r:8 * r + 8, :] for r in range(TOPK)], SORT16)
    for shift in (4, 2, 1):
        v = _exchange([jnp.maximum(v[r], pltpu.roll(v[TOPK - 1 - r], shift, axis=0)) for r in range(TOPK)],
                      BITONIC16)
    return v


def _topk_kernel(xt_ref, wq_ref, keys_ref, r2_ref, e2_ref, n1_ref, c1_ref,
                 qt_ref, s_ref, a_ref, b_ref, cnt_ref, iz_ref, r1_ref, *, tm):
    NH, K = PEER_HEADS, PEER_KEYS
    qt_ref[...] = jnp.dot(wq_ref[...], xt_ref[...], preferred_element_type=F32).astype(BF16)
    for hp in range(2 * NH):
        s_ref[hp] = jnp.dot(keys_ref[hp], qt_ref[hp * PEER_HALF:(hp + 1) * PEER_HALF, :],
                            preferred_element_type=F32)

    pairs = [(i, j) for i in range(TOPK) for j in range(TOPK) if (i + 1) * (j + 1) <= TOPK]
    key_id = lax.broadcasted_iota(jnp.int32, (K, LANES), 0).astype(F32)

    def select_stats(lanes, tie_break):
        bad = jnp.zeros((1, LANES), F32)
        for h in range(NH):
            for p in range(2):
                cur = s_ref[2 * h + p, :, lanes]
                top_ref = a_ref if p == 0 else b_ref
                rank = jnp.full(cur.shape, NO_RANK, F32)
                if tie_break:
                    for j in range(TOPK):
                        m = jnp.max(cur, axis=0, keepdims=True)
                        first = jnp.min(jnp.where(cur == m, key_id, float(K)), axis=0, keepdims=True)
                        hit = key_id == first
                        top_ref[j, h:h + 1, :] = m
                        rank = jnp.where(hit, float(j), rank)
                        cur = jnp.where(hit, -jnp.inf, cur)
                    if p == 0:
                        r1_ref[h] = rank
                else:
                    top = _top16_sorted(cur)
                    for j in range(TOPK):
                        top_ref[j, h:h + 1, :] = top[j][0:1, :]
                    dup = functools.reduce(jnp.logical_or, [top[j] == top[j + 1] for j in range(TOPK - 1)])
                    reach = functools.reduce(jnp.add, [jnp.where(cur[8 * r:8 * r + 8, :] >= top[TOPK - 1], 1.0, 0.0)
                                                       for r in range(K // 8)])
                    reach = jnp.sum(reach, axis=0, keepdims=True)
                    tied = jnp.logical_or(dup[0:1, :], reach != float(TOPK))
                    bad = jnp.maximum(bad, jnp.where(tied, 1.0, 0.0))
                    if p == 1:
                        for j in reversed(range(TOPK)):
                            rank = jnp.where(cur >= jnp.concatenate([top[j]] * (K // 8), axis=0), float(j), rank)
                if p == 1:
                    r2_ref[h, :, lanes] = rank.astype(r2_ref.dtype)
        a = [a_ref[i] for i in range(TOPK)]
        b = [b_ref[j] for j in range(TOPK)]
        row = [[a[i] + b[j] for j in range(TOPK // (i + 1))] for i in range(TOPK)]
        cand = [row[i][j] for (i, j) in pairs]
        if tie_break:
            sel = []
            for ip, cp in enumerate(cand):
                ahead = jnp.zeros(cp.shape, F32)
                for iq, cq in enumerate(cand):
                    if iq != ip:
                        ahead = ahead + jnp.where((cq >= cp) if iq < ip else (cq > cp), 1.0, 0.0)
                sel.append(ahead < float(TOPK))
        else:
            tau = _kth_pair_sum(row)
            sel = [c >= tau for c in cand]
        zsum = jnp.zeros(cand[0].shape, F32)
        cnt = [jnp.zeros(cand[0].shape, F32) for _ in range(TOPK)]
        for (i, j), cp, sp in zip(pairs, cand, sel):
            cnt[i] = cnt[i] + jnp.where(sp, 1.0, 0.0)
            zsum = zsum + jnp.where(sp, jnp.exp(cp - cand[0]), 0.0)
        for i in range(TOPK):
            cnt_ref[i] = cnt[i]
        iz_ref[...] = 0.5 / zsum
        if not tie_break:
            total = functools.reduce(jnp.add, cnt)
            bad8 = jnp.where(total != float(TOPK), 1.0, 0.0)
            bad = jnp.maximum(bad, jnp.max(bad8, axis=0, keepdims=True))
        for h in range(NH):
            s1 = s_ref[2 * h, :, lanes]
            n1 = jnp.zeros(s1.shape, F32)
            for i in range(TOPK):
                hit = (r1_ref[h] == float(i)) if tie_break else (s1 == a_ref[i, h:h + 1, :])
                n1 = jnp.where(hit, cnt_ref[i, h:h + 1, :], n1)
            n1_ref[h, :, lanes] = n1
            c1_ref[h, :, lanes] = jnp.exp(s1 - a_ref[0, h:h + 1, :]) * iz_ref[h:h + 1, :]
            s2 = s_ref[2 * h + 1, :, lanes]
            e2_ref[h, :, lanes] = jnp.exp(s2 - b_ref[0, h:h + 1, :]).astype(e2_ref.dtype)
        return bad

    def group(gi, carry):
        lanes = pl.ds(pl.multiple_of(gi * LANES, LANES), LANES)
        bad = select_stats(lanes, tie_break=False)

        @pl.when(jnp.max(bad) > 0.0)
        def _():
            select_stats(lanes, tie_break=True)

        return carry

    lax.fori_loop(0, tm // LANES, group, 0)


def _topk(xt, wq_t, keys, tm):
    D, T = xt.shape
    NH, K = PEER_HEADS, PEER_KEYS
    stat = pl.BlockSpec((NH, K, tm), lambda i: (0, 0, i))
    return pl.pallas_call(
        functools.partial(_topk_kernel, tm=tm),
        grid=(T // tm,),
        in_specs=[pl.BlockSpec((D, tm), lambda i: (0, i)),
                  pl.BlockSpec(wq_t.shape, lambda i: (0, 0)),
                  pl.BlockSpec(keys.shape, lambda i: (0, 0, 0))],
        out_specs=[stat] * 4,
        out_shape=[jax.ShapeDtypeStruct((NH, K, T), BF16), jax.ShapeDtypeStruct((NH, K, T), BF16),
                   jax.ShapeDtypeStruct((NH, K, T), F32), jax.ShapeDtypeStruct((NH, K, T), F32)],
        scratch_shapes=[pltpu.VMEM((wq_t.shape[0], tm), BF16),
                        pltpu.VMEM((2 * NH, K, tm), F32),
                        pltpu.VMEM((TOPK, NH, LANES), F32),
                        pltpu.VMEM((TOPK, NH, LANES), F32),
                        pltpu.VMEM((TOPK, NH, LANES), F32),
                        pltpu.VMEM((NH, LANES), F32),
                        pltpu.VMEM((NH, K, LANES), F32)],
        compiler_params=_params(("parallel",)),
        name="peer_topk",
    )(xt, wq_t, keys)


def _gelu2(x):
    return x + x * jnp.tanh(x * (GELU_C0 + GELU_C1 * (x * x)))


def _peer_kernel(xt_ref, u_ref, vt_ref, r2_ref, e2_ref, n1_ref, c1_ref, x1_ref, fg_ref, o_ref,
                 acc_ref, pt_ref, at_ref, *, te, tm, nsplit):
    NH, K = PEER_HEADS, PEER_KEYS
    e = pl.program_id(1)

    @pl.when(e == 0)
    def _():
        acc_ref[...] = jnp.zeros_like(acc_ref)

    n_i1 = te // K
    i1_rows = pl.ds(pl.multiple_of(e * n_i1, n_i1), n_i1)
    sub = te // nsplit

    def scores(s):
        at_ref[s] = jnp.dot(u_ref[s * sub:(s + 1) * sub, :], xt_ref[...], preferred_element_type=F32)

    def gated(s):
        nk = sub // K
        for lg in range(tm // LANES):
            lanes = slice(lg * LANES, (lg + 1) * LANES)
            gates = [None] * nk
            for h in range(NH):
                r2 = r2_ref[h, :, lanes]
                e2 = e2_ref[h, :, lanes]
                n1t = n1_ref[h, i1_rows, lanes]
                c1t = c1_ref[h, i1_rows, lanes]
                for k in range(nk):
                    kk = s * nk + k
                    n1 = jnp.broadcast_to(n1t[kk:kk + 1, :], (K, LANES)).astype(BF16)
                    c1 = jnp.broadcast_to(c1t[kk:kk + 1, :], (K, LANES)).astype(BF16)
                    term = jnp.minimum(jnp.maximum(n1 - r2, 0), e2) * c1
                    gates[k] = term if gates[k] is None else gates[k] + term
            for k in range(nk):
                kk = s * nk + k
                act = _gelu2(at_ref[s, k * K:(k + 1) * K, lanes]).astype(BF16)
                pt_ref[kk * K:(kk + 1) * K, lanes] = act * gates[k]

    def mixed(s0, ns):
        rows = slice(s0 * sub, (s0 + ns) * sub)
        for dm in range(0, acc_ref.shape[0], OUT_CHUNK):
            out = slice(dm, dm + OUT_CHUNK)
            acc_ref[out, :] += jnp.dot(vt_ref[out, rows], pt_ref[rows, :], preferred_element_type=F32)

    assert nsplit % MIX_GROUP == 0
    for s in range(nsplit):
        scores(s)
    for s in range(nsplit):
        gated(s)
        if s % MIX_GROUP == MIX_GROUP - 1:
            mixed(s - MIX_GROUP + 1, MIX_GROUP)

    @pl.when(e == pl.num_programs(1) - 1)
    def _():
        o_ref[...] = _rms(x1_ref[...] + acc_ref[...].T, fg_ref[...])


def _peer(xt, u, vt, r2, e2, n1, c1, x1, fg, tm, te, nsplit=8):
    D, T = xt.shape
    E = u.shape[0]
    NH, K = PEER_HEADS, PEER_KEYS
    assert te % (8 * K) == 0, "a step must cover whole sublane tiles of first-half keys"
    stat = pl.BlockSpec((NH, K, tm), lambda i, e: (0, 0, i))
    return pl.pallas_call(
        functools.partial(_peer_kernel, te=te, tm=tm, nsplit=nsplit),
        grid=(T // tm, E // te),
        in_specs=[pl.BlockSpec((D, tm), lambda i, e: (0, i)),
                  pl.BlockSpec((te, D), lambda i, e: (e, 0)),
                  pl.BlockSpec((D, te), lambda i, e: (0, e)),
                  stat, stat, stat, stat,
                  pl.BlockSpec((tm, D), lambda i, e: (i, 0)),
                  pl.BlockSpec((1, D), lambda i, e: (0, 0))],
        out_specs=pl.BlockSpec((tm, D), lambda i, e: (i, 0)),
        out_shape=jax.ShapeDtypeStruct((T, D), F32),
        scratch_shapes=[pltpu.VMEM((D, tm), F32), pltpu.VMEM((te, tm), BF16),
                        pltpu.VMEM((nsplit, te // nsplit, tm), F32)],
        compiler_params=_params(("parallel", "arbitrary")),
        name="peer_dense",
    )(xt, u, vt, r2, e2, n1, c1, x1, fg)


def _tiles(T, S):
    def pick(n, pref):
        t = min(pref, n)
        while n % t:
            t //= 2
        return t
    return dict(tm_in=pick(T, 1024), tn_in=1536, nck=pick(S // CHUNK, 16), tg=pick(T, 512),
                tm_out=pick(T, 512), tm_topk=pick(T, 512), tm_peer=pick(T, 512), te=16 * PEER_KEYS)


def _layer(x2d, B, S, norm1_g, w_in, conv_w, conv_b, b_igate, b_fgate, mlstm_norm_g,
           gmlp_ln_g, gmlp_ln_b, gmlp_w_s, gmlp_b_s, w_out, norm2_g,
           peer_w_query, peer_sub_keys, peer_u, peer_v, out_g):
    T, D = x2d.shape
    H, L, W = HEADS, CHUNK, MLSTM_W
    nc = S // L
    t = _tiles(T, S)

    g0 = 4 * W
    u0 = g0 + 2 * H
    w_main = jnp.concatenate([w_in[:, :g0], w_in[:, u0:]], axis=1).astype(BF16)
    w_gate = jnp.pad(w_in[:, g0:u0], ((0, 0), (0, LANES - 2 * H))).astype(BF16)
    z, zg = _inproj(x2d, norm1_g[None, :], w_main, w_gate, t["tm_in"], t["tn_in"])

    gcols = zg[:, :2 * H].reshape(B, nc, L, 2, H).transpose(3, 2, 0, 4, 1).reshape(2, L, B * H * nc)
    bi_row = jnp.broadcast_to(b_igate[None, :, None], (B, H, nc)).reshape(1, -1)
    bf_row = jnp.broadcast_to(b_fgate[None, :, None], (B, H, nc)).reshape(1, -1)
    cola, sc, emt, wg, r, decay = _gates(gcols[0], gcols[1], bi_row, bf_row, nc)
    cs = jnp.stack([cola, sc, emt, wg]).reshape(4, L, B, H, nc).transpose(2, 4, 1, 0, 3).reshape(B, nc, L, 4 * H)
    r_rows = jnp.pad(r.reshape(L, B, H, nc).transpose(1, 3, 2, 0), ((0, 0), (0, 0), (0, 0), (0, 2 * HEAD_DIM - L)))
    d_rows = jnp.broadcast_to(decay[0].reshape(B, H, nc).transpose(0, 2, 1)[..., None], (B, nc, H, 2 * HEAD_DIM))
    rs = jnp.concatenate([r_rows, d_rows], axis=2)
    ya = _mlstm(z, cs, rs, conv_w, conv_b[None, :], mlstm_norm_g[None, :], B, S, t["nck"])

    bs_b = jnp.broadcast_to(gmlp_b_s[:, :, None], (GROUPS, GMLP_CHUNK, GROUP_DIM))
    yb = _gmlp(z, gmlp_ln_g[None, :], gmlp_ln_b[None, :], gmlp_w_s, bs_b, t["tg"])

    w_o = w_out.astype(BF16)
    x1, xt = _outproj(x2d, ya, yb, w_o[:W], w_o[W:], norm2_g[None, :], t["tm_out"])

    wq_t = peer_w_query.T.astype(BF16)
    keys = peer_sub_keys.reshape(2 * PEER_HEADS, PEER_KEYS, PEER_HALF).astype(BF16)
    r2, e2, n1, c1 = _topk(xt, wq_t, keys, t["tm_topk"])
    return _peer(xt, peer_u.astype(BF16), peer_v.T.astype(BF16), r2, e2, n1, c1, x1,
                 out_g[None, :], t["tm_peer"], t["te"])


def kernel(x, norm1_g, w_in, conv_w, conv_b, b_igate, b_fgate, mlstm_norm_g, gmlp_ln_g, gmlp_ln_b,
           gmlp_w_s, gmlp_b_s, w_out, norm2_g, peer_w_query, peer_sub_keys, peer_u, peer_v, final_g):
    B, S, D = x.shape
    depth = norm1_g.shape[0]
    assert depth == 1, "final rmsnorm is fused into the last layer's PEER kernel"
    x2d = x.reshape(B * S, D)
    out = _layer(x2d, B, S, norm1_g[0], w_in[0], conv_w[0], conv_b[0], b_igate[0], b_fgate[0],
                 mlstm_norm_g[0], gmlp_ln_g[0], gmlp_ln_b[0], gmlp_w_s[0], gmlp_b_s[0], w_out[0],
                 norm2_g[0], peer_w_query[0], peer_sub_keys[0], peer_u[0], peer_v[0], final_g)
    return out.reshape(B, S, D)
```
